```python
import jax, jax.numpy as jnp
from jax import lax
import numpy as np

D_MODEL = 1024
BATCH = 4
SEQ = 8192
DEPTH = 2
DEC_BATCH = 128
DEC_SEQ = 4
PAST_LEN = 16384
PAGE_SIZE = 128

HEAD_DIM = 64
N_HEADS = D_MODEL // HEAD_DIM
N_KV_SB = 4
N_KV_SW = 4
G_SB = N_HEADS // N_KV_SB
G_SW = N_HEADS // N_KV_SW
WINDOW = 128
ROT_DIM = HEAD_DIM // 4
ROPE_THETA = 500000.0
D_FF = -(-8 * D_MODEL // (3 * 256)) * 256
Q_BLOCK = 128
EPS = 1e-6
SB_BIAS_INIT = -8.0
N_SB_LAYERS = (DEPTH + 1) // 2
N_SW_LAYERS = DEPTH // 2
QKV_SB = (N_HEADS + 2 * N_KV_SB) * HEAD_DIM
QKV_SW = (N_HEADS + 2 * N_KV_SW) * HEAD_DIM

kernel_name = "stick_breaking_swa_sink_hybrid_step"


def rms_norm(x, g):
    xf = x.astype(jnp.float32)
    y = xf * lax.rsqrt(jnp.mean(xf * xf, axis=-1, keepdims=True) + EPS)
    return (y * g.astype(jnp.float32)).astype(x.dtype)


def rope_partial(x, pos):
    half = ROT_DIM // 2
    inv_freq = ROPE_THETA ** (-jnp.arange(half, dtype=jnp.float32) * 2.0 / ROT_DIM)
    ang = pos.astype(jnp.float32)[:, None] * inv_freq[None, :]
    cos = jnp.cos(ang)[:, None, :]
    sin = jnp.sin(ang)[:, None, :]
    xf = x.astype(jnp.float32)
    x1 = xf[..., :half]
    x2 = xf[..., half:ROT_DIM]
    out = jnp.concatenate([x1 * cos - x2 * sin, x2 * cos + x1 * sin, xf[..., ROT_DIM:]], axis=-1)
    return out.astype(x.dtype)


def split_qkv(h, w_qkv, n_kv):
    b, t, _ = h.shape
    qkv = h @ w_qkv
    nq = N_HEADS * HEAD_DIM
    nk = n_kv * HEAD_DIM
    q = qkv[..., :nq].reshape(b, t, N_HEADS, HEAD_DIM)
    k = qkv[..., nq:nq + nk].reshape(b, t, n_kv, HEAD_DIM)
    v = qkv[..., nq + nk:].reshape(b, t, n_kv, HEAD_DIM)
    return q, k, v


def swiglu(h, w_gate_up, w_down):
    g, u = jnp.split(h @ w_gate_up, 2, axis=-1)
    return (jax.nn.silu(g) * u) @ w_down


def stick_breaking_attn(q, k, v, bias, qoff):
    z = jnp.einsum('nqhgd,nkhd->nhgqk', q, k, preferred_element_type=jnp.float32) * (HEAD_DIM ** -0.5)
    z = z + bias.astype(jnp.float32)[None, :, :, None, None]
    qi = qoff + jnp.arange(q.shape[1])
    kj = jnp.arange(k.shape[1])
    causal = kj[None, :] < qi[:, None]
    log_beta = jax.nn.log_sigmoid(z)
    log_1mb = jnp.where(causal, jax.nn.log_sigmoid(-z), 0.0)
    later = lax.cumsum(log_1mb, axis=4, reverse=True) - log_1mb
    a = jnp.where(causal, jnp.exp(log_beta + later), 0.0)
    return jnp.einsum('nhgqk,nkhd->nqhgd', a.astype(v.dtype), v)


def sink_window_attn(q, k, v, sinks, qoff, key_valid):
    s = jnp.einsum('nqhgd,nkhd->nhgqk', q, k, preferred_element_type=jnp.float32) * (HEAD_DIM ** -0.5)
    qi = qoff + jnp.arange(q.shape[1])
    kj = jnp.arange(k.shape[1])
    band = (kj[None, :] <= qi[:, None]) & (kj[None, :] >= qi[:, None] - WINDOW)
    mask = band[None, None, None] & key_valid[:, None, None, None, :]
    s = jnp.where(mask, s, -jnp.inf)
    sink = sinks.astype(jnp.float32)[None, :, :, None, None]
    m = jnp.maximum(jnp.max(s, axis=-1, keepdims=True), sink)
    p = jnp.exp(s - m)
    w = p / (jnp.sum(p, axis=-1, keepdims=True) + jnp.exp(sink - m))
    return jnp.einsum('nhgqk,nkhd->nqhgd', w.astype(v.dtype), v)


def band_blocks(t):
    b, s, h, d = t.shape
    nb = s // Q_BLOCK
    tb = t.reshape(b, nb, Q_BLOCK, h, d)
    prev = jnp.pad(tb, ((0, 0), (1, 0), (0, 0), (0, 0), (0, 0)))[:, :-1]
    return jnp.concatenate([prev, tb], axis=2).reshape(b * nb, 2 * Q_BLOCK, h, d)


def sb_prompt(h, w_qkv, bias):
    b, s, _ = h.shape
    q, k, v = split_qkv(h, w_qkv, N_KV_SB)
    nb = s // Q_BLOCK
    qb = q.reshape(b, nb, Q_BLOCK, N_KV_SB, G_SB, HEAD_DIM).swapaxes(0, 1)
    bias_g = bias.reshape(N_KV_SB, G_SB)

    def block(args):
        q_blk, bi = args
        return stick_breaking_attn(q_blk, k, v, bias_g, bi * Q_BLOCK)

    o = lax.map(block, (qb, jnp.arange(nb)))
    o = o.swapaxes(0, 1).reshape(b, s, N_HEADS * HEAD_DIM)
    return o, k, v


def sb_sample(h, w_qkv, bias, cache_k, cache_v, page_table, layer):
    b, t, _ = h.shape
    q, k, v = split_qkv(h, w_qkv, N_KV_SB)
    qg = q.reshape(b, t, N_KV_SB, G_SB, HEAD_DIM)
    bias_g = bias.reshape(N_KV_SB, G_SB)

    def one_seq(args):
        q_i, k_i, v_i, pages = args
        kk = jnp.concatenate([cache_k[layer, pages].reshape(-1, N_KV_SB, HEAD_DIM), k_i], axis=0)
        vv = jnp.concatenate([cache_v[layer, pages].reshape(-1, N_KV_SB, HEAD_DIM), v_i], axis=0)
        return stick_breaking_attn(q_i[None], kk[None], vv[None], bias_g, kk.shape[0] - q_i.shape[0])[0]

    o = lax.map(one_seq, (qg, k, v, page_table))
    return o.reshape(b, t, N_HEADS * HEAD_DIM), k, v


def sw_qk(q, k, q_norm, k_norm, pos):
    return rope_partial(rms_norm(q, q_norm), pos), rope_partial(rms_norm(k, k_norm), pos)


def sw_prompt(h, w_qkv, q_norm, k_norm, sinks):
    b, s, _ = h.shape
    q, k, v = split_qkv(h, w_qkv, N_KV_SW)
    q, k = sw_qk(q, k, q_norm, k_norm, jnp.arange(s))
    nb = s // Q_BLOCK
    qb = q.reshape(b * nb, Q_BLOCK, N_KV_SW, G_SW, HEAD_DIM)
    valid = (jnp.arange(2 * Q_BLOCK)[None, :] >= Q_BLOCK) | (jnp.arange(nb)[:, None] > 0)
    valid = jnp.broadcast_to(valid[None], (b, nb, 2 * Q_BLOCK)).reshape(b * nb, 2 * Q_BLOCK)
    o = sink_window_attn(qb, band_blocks(k), band_blocks(v), sinks.reshape(N_KV_SW, G_SW), Q_BLOCK, valid)
    o = o.reshape(b, s, N_HEADS * HEAD_DIM)
    keep = min(WINDOW, s)
    return o, k[:, -keep:], v[:, -keep:]


def sw_sample(h, w_qkv, q_norm, k_norm, sinks, buf_k, buf_v):
    b, t, _ = h.shape
    q, k, v = split_qkv(h, w_qkv, N_KV_SW)
    q, k = sw_qk(q, k, q_norm, k_norm, PAST_LEN + jnp.arange(t))
    n_buf = buf_k.shape[1]
    kk = jnp.concatenate([buf_k, k], axis=1)
    vv = jnp.concatenate([buf_v, v], axis=1)
    valid = jnp.ones((b, kk.shape[1]), dtype=bool)
    o = sink_window_attn(q.reshape(b, t, N_KV_SW, G_SW, HEAD_DIM), kk, vv,
                         sinks.reshape(N_KV_SW, G_SW), n_buf, valid)
    return o.reshape(b, t, N_HEADS * HEAD_DIM), kk[:, -n_buf:], vv[:, -n_buf:]


def setup_inputs(seed: int = 0) -> dict:
    key = jax.random.key(seed)
    ks = jax.random.split(key, 24)
    n_pages = PAST_LEN // PAGE_SIZE
    n_used = DEC_BATCH * n_pages
    n_phys = n_used + max(1, n_used // 4)
    win_buf = min(WINDOW, PAST_LEN)
    f32 = jnp.float32

    def nrm(k, shape, scale=1.0):
        return jax.random.normal(k, shape, dtype=f32) * scale

    page_table = jax.random.permutation(ks[0], n_phys)[:n_used].reshape(DEC_BATCH, n_pages).astype(jnp.int32)
    return {
        "x_prompt": nrm(ks[1], (BATCH, SEQ, D_MODEL)),
        "x_sample": nrm(ks[2], (DEC_BATCH, DEC_SEQ, D_MODEL)),
        "cache_k_sb": nrm(ks[3], (N_SB_LAYERS, n_phys, PAGE_SIZE, N_KV_SB, HEAD_DIM)),
        "cache_v_sb": nrm(ks[4], (N_SB_LAYERS, n_phys, PAGE_SIZE, N_KV_SB, HEAD_DIM)),
        "page_table": page_table,
        "state_k_win": nrm(ks[5], (N_SW_LAYERS, DEC_BATCH, win_buf, N_KV_SW, HEAD_DIM)),
        "state_v_win": nrm(ks[6], (N_SW_LAYERS, DEC_BATCH, win_buf, N_KV_SW, HEAD_DIM)),
        "sb_norm": 1.0 + nrm(ks[7], (N_SB_LAYERS, D_MODEL), 0.05),
        "sb_w_qkv": nrm(ks[8], (N_SB_LAYERS, D_MODEL, QKV_SB), D_MODEL ** -0.5),
        "sb_bias": SB_BIAS_INIT + nrm(ks[19], (N_SB_LAYERS, N_HEADS), 0.1),
        "sb_w_o": nrm(ks[9], (N_SB_LAYERS, N_HEADS * HEAD_DIM, D_MODEL), (N_HEADS * HEAD_DIM) ** -0.5),
        "sw_norm": 1.0 + nrm(ks[10], (N_SW_LAYERS, D_MODEL), 0.05),
        "sw_w_qkv": nrm(ks[11], (N_SW_LAYERS, D_MODEL, QKV_SW), D_MODEL ** -0.5),
        "sw_q_norm": 1.0 + nrm(ks[12], (N_SW_LAYERS, HEAD_DIM), 0.05),
        "sw_k_norm": 1.0 + nrm(ks[13], (N_SW_LAYERS, HEAD_DIM), 0.05),
        "sw_sinks": nrm(ks[14], (N_SW_LAYERS, N_HEADS), 0.5),
        "sw_w_o": nrm(ks[15], (N_SW_LAYERS, N_HEADS * HEAD_DIM, D_MODEL), (N_HEADS * HEAD_DIM) ** -0.5),
        "ffn_norm": 1.0 + nrm(ks[16], (DEPTH, D_MODEL), 0.05),
        "ffn_w_gate_up": nrm(ks[17], (DEPTH, D_MODEL, 2 * D_FF), D_MODEL ** -0.5),
        "ffn_w_down": nrm(ks[18], (DEPTH, D_FF, D_MODEL), D_FF ** -0.5),
    }


def reference(x_prompt, x_sample, cache_k_sb, cache_v_sb, page_table, state_k_win, state_v_win,
              sb_norm, sb_w_qkv, sb_bias, sb_w_o, sw_norm, sw_w_qkv, sw_q_norm, sw_k_norm, sw_sinks, sw_w_o,
              ffn_norm, ffn_w_gate_up, ffn_w_down):
    hp, hs = x_prompt, x_sample
    sb_kp, sb_vp, sb_ks, sb_vs = [], [], [], []
    sw_kp, sw_vp, sw_ks, sw_vs = [], [], [], []
    for i in range(DEPTH):
        j = i // 2
        if i % 2 == 0:
            op, kp, vp = sb_prompt(rms_norm(hp, sb_norm[j]), sb_w_qkv[j], sb_bias[j])
            os_, ks_, vs_ = sb_sample(rms_norm(hs, sb_norm[j]), sb_w_qkv[j], sb_bias[j],
                                      cache_k_sb, cache_v_sb, page_table, j)
            hp = hp + op @ sb_w_o[j]
            hs = hs + os_ @ sb_w_o[j]
            sb_kp.append(kp); sb_vp.append(vp); sb_ks.append(ks_); sb_vs.append(vs_)
        else:
            op, kp, vp = sw_prompt(rms_norm(hp, sw_norm[j]), sw_w_qkv[j], sw_q_norm[j], sw_k_norm[j], sw_sinks[j])
            os_, ks_, vs_ = sw_sample(rms_norm(hs, sw_norm[j]), sw_w_qkv[j], sw_q_norm[j], sw_k_norm[j], sw_sinks[j],
                                      state_k_win[j], state_v_win[j])
            hp = hp + op @ sw_w_o[j]
            hs = hs + os_ @ sw_w_o[j]
            sw_kp.append(kp); sw_vp.append(vp); sw_ks.append(ks_); sw_vs.append(vs_)
        hp = hp + swiglu(rms_norm(hp, ffn_norm[i]), ffn_w_gate_up[i], ffn_w_down[i])
        hs = hs + swiglu(rms_norm(hs, ffn_norm[i]), ffn_w_gate_up[i], ffn_w_down[i])
    return (hp, hs,
            jnp.stack(sb_kp), jnp.stack(sb_vp), jnp.stack(sb_ks), jnp.stack(sb_vs),
            jnp.stack(sw_kp), jnp.stack(sw_vp), jnp.stack(sw_ks), jnp.stack(sw_vs))
```

```python
import functools

import jax
import jax.numpy as jnp
from jax import lax
from jax.experimental import pallas as pl
from jax.experimental.pallas import tpu as pltpu

F32 = jnp.float32
BF16 = jnp.bfloat16

HEAD_DIM = 64
N_HEADS = 16
N_KV = 4
GROUP = N_HEADS // N_KV
KV_WIDTH = N_KV * HEAD_DIM
Q_WIDTH = N_HEADS * HEAD_DIM
WINDOW = 128
ROT_DIM = HEAD_DIM // 4
ROPE_THETA = 500000.0
EPS = 1e-6
PAGE_SIZE = 128
SCALE = HEAD_DIM ** -0.5

LANES = 128
BLK = 128
ROW_TILE = 512
VMEM_LIMIT = 56 * 1024 * 1024
PAGES_PER_CHUNK = 8
SEQS_PER_STEP = 8


def _params(sem):
    return pltpu.CompilerParams(dimension_semantics=sem, vmem_limit_bytes=VMEM_LIMIT)


def _const_spec(shape):
    nd = len(shape)
    return pl.BlockSpec(shape, lambda *_: (0,) * nd, pipeline_mode=pl.Buffered(1))


def _rms_norm_rows(x, gain):
    ms = jnp.mean(x * x, axis=-1, keepdims=True)
    return x * lax.rsqrt(ms + EPS) * gain


def _softplus(z):
    return jnp.maximum(z, 0.0) + jnp.log(1.0 + jnp.exp(-jnp.abs(z)))


def _dot_nt(a, b):
    return lax.dot_general(a, b, (((1,), (1,)), ((), ())), preferred_element_type=F32)


def _dot(a, b):
    return jnp.dot(a, b, preferred_element_type=F32)


def _store_head_major(dst_ref, x):
    for h in range(N_KV):
        dst_ref[h] = x[:, h * HEAD_DIM:(h + 1) * HEAD_DIM].astype(BF16)


def _sb_qkv_kernel(x_ref, g_ref, w_ref, q_ref, k_ref, v_ref, khm_ref, vhm_ref):
    h = _rms_norm_rows(x_ref[...], g_ref[...]).astype(BF16)
    qkv = _dot(h, w_ref[...])
    q_ref[...] = (qkv[:, :Q_WIDTH] * SCALE).astype(BF16)
    k = qkv[:, Q_WIDTH:Q_WIDTH + KV_WIDTH]
    v = qkv[:, Q_WIDTH + KV_WIDTH:]
    k_ref[...] = k
    v_ref[...] = v
    _store_head_major(khm_ref, k)
    _store_head_major(vhm_ref, v)


def _split_bf16(x):
    hi = x.astype(BF16)
    lo = (x - hi.astype(F32)).astype(BF16)
    return hi, lo


def _sw_qkv_kernel(x_ref, g_ref, w_ref, gain_ref, psum_ref, pexp_ref, cos_ref, s1_ref, s2_ref,
                   q_ref, k_ref, v_ref, khm_ref, vhm_ref):
    h = _rms_norm_rows(x_ref[...], g_ref[...]).astype(BF16)
    qkv = _dot(h, w_ref[...])
    qk_width = Q_WIDTH + KV_WIDTH
    qk = qkv[:, :qk_width]
    v = qkv[:, qk_width:]
    sq_hi, sq_lo = _split_bf16(qk * qk)
    ss = _dot(sq_hi, psum_ref[...]) + _dot(sq_lo, psum_ref[...])
    inv_hi, inv_lo = _split_bf16(lax.rsqrt(ss * (1.0 / HEAD_DIM) + EPS))
    inv = _dot(inv_hi, pexp_ref[...]) + _dot(inv_lo, pexp_ref[...])
    qkn = qk * inv * gain_ref[...]
    cos, s1, s2 = cos_ref[...], s1_ref[...], s2_ref[...]
    half = ROT_DIM // 2
    for c in range(qk_width // LANES):
        blk = qkn[:, c * LANES:(c + 1) * LANES]
        rot = (blk * cos + pltpu.roll(blk, half, axis=1) * s1
               + pltpu.roll(blk, LANES - half, axis=1) * s2)
        if c < Q_WIDTH // LANES:
            q_ref[:, c * LANES:(c + 1) * LANES] = (rot * SCALE).astype(BF16)
        else:
            j = c - Q_WIDTH // LANES
            k_ref[:, j * LANES:(j + 1) * LANES] = rot
            khm_ref[2 * j] = rot[:, :HEAD_DIM].astype(BF16)
            khm_ref[2 * j + 1] = rot[:, HEAD_DIM:].astype(BF16)
    v_ref[...] = v
    _store_head_major(vhm_ref, v)


def _qkv_out(n_rows, tm):
    shapes = (jax.ShapeDtypeStruct((n_rows, Q_WIDTH), BF16),
              jax.ShapeDtypeStruct((n_rows, KV_WIDTH), F32),
              jax.ShapeDtypeStruct((n_rows, KV_WIDTH), F32),
              jax.ShapeDtypeStruct((N_KV, n_rows, HEAD_DIM), BF16),
              jax.ShapeDtypeStruct((N_KV, n_rows, HEAD_DIM), BF16))
    specs = (pl.BlockSpec((tm, Q_WIDTH), lambda i: (i, 0)),
             pl.BlockSpec((tm, KV_WIDTH), lambda i: (i, 0)),
             pl.BlockSpec((tm, KV_WIDTH), lambda i: (i, 0)),
             pl.BlockSpec((N_KV, tm, HEAD_DIM), lambda i: (0, i, 0)),
             pl.BlockSpec((N_KV, tm, HEAD_DIM), lambda i: (0, i, 0)))
    return shapes, specs


def _row_tile(n_rows):
    return min(ROW_TILE, n_rows)


def _sb_qkv(x, gain, w):
    n_rows, d = x.shape
    tm = _row_tile(n_rows)
    shapes, specs = _qkv_out(n_rows, tm)
    return pl.pallas_call(
        _sb_qkv_kernel,
        grid=(n_rows // tm,),
        in_specs=[pl.BlockSpec((tm, d), lambda i: (i, 0)),
                  _const_spec((1, d)),
                  _const_spec(w.shape)],
        out_specs=specs, out_shape=shapes,
        compiler_params=_params(("arbitrary",)),
        name="sb_qkv",
    )(x, gain.reshape(1, d), w)


def _rope_tables(pos):
    half = ROT_DIM // 2
    inv_freq = ROPE_THETA ** (-jnp.arange(half, dtype=F32) * 2.0 / ROT_DIM)
    ang = pos.astype(F32)[:, None] * inv_freq[None, :]
    cos, sin = jnp.cos(ang), jnp.sin(ang)
    t = pos.shape[0]
    ones = jnp.ones((t, HEAD_DIM - ROT_DIM), F32)
    zeros_h = jnp.zeros((t, half), F32)
    zeros_r = jnp.zeros((t, HEAD_DIM - ROT_DIM), F32)
    c_head = jnp.concatenate([cos, cos, ones], axis=1)
    s1_head = jnp.concatenate([zeros_h, sin, zeros_r], axis=1)
    s2_head = jnp.concatenate([-sin, zeros_h, zeros_r], axis=1)
    two = lambda a: jnp.concatenate([a, a], axis=1)
    return two(c_head), two(s1_head), two(s2_head)


def _sw_qkv(x, gain, w, q_norm, k_norm, pos_tables, table_blocks):
    n_rows, d = x.shape
    tm = _row_tile(n_rows)
    shapes, specs = _qkv_out(n_rows, tm)
    qk_width = Q_WIDTH + KV_WIDTH
    n_qk_heads = qk_width // HEAD_DIM
    head_of_lane = jnp.arange(qk_width) // HEAD_DIM
    psum = (head_of_lane[:, None] == jnp.arange(LANES)[None, :]).astype(BF16)
    pexp = psum.T
    gain_qk = jnp.concatenate([jnp.tile(q_norm, N_HEADS), jnp.tile(k_norm, N_KV)]).reshape(1, qk_width)
    del n_qk_heads
    tab_spec = pl.BlockSpec((tm, LANES), lambda i: (i % table_blocks, 0))
    return pl.pallas_call(
        _sw_qkv_kernel,
        grid=(n_rows // tm,),
        in_specs=[pl.BlockSpec((tm, d), lambda i: (i, 0)),
                  _const_spec((1, d)),
                  _const_spec(w.shape),
                  _const_spec((1, qk_width)),
                  _const_spec(psum.shape),
                  _const_spec(pexp.shape),
                  tab_spec, tab_spec, tab_spec],
        out_specs=specs, out_shape=shapes,
        compiler_params=_params(("arbitrary",)),
        name="sw_qkv",
    )(x, gain.reshape(1, d), w, gain_qk.astype(F32), psum, pexp, *pos_tables)


def _oproj_ffn_kernel(h_ref, o_ref, wo_ref, g_ref, wgu_ref, wd_ref, out_ref, *, d_ff, ff_chunk):
    h1 = h_ref[...] + _dot(o_ref[...], wo_ref[...])
    n = _rms_norm_rows(h1, g_ref[...]).astype(BF16)
    acc = h1
    for c in range(d_ff // ff_chunk):
        lo = c * ff_chunk
        gate = _dot(n, wgu_ref[:, lo:lo + ff_chunk])
        up = _dot(n, wgu_ref[:, d_ff + lo:d_ff + lo + ff_chunk])
        act = (gate * jax.nn.sigmoid(gate) * up).astype(BF16)
        acc = acc + _dot(act, wd_ref[lo:lo + ff_chunk, :])
    out_ref[...] = acc


def _ff_chunk(d_ff):
    for n_chunks in (2, 1):
        if d_ff % (n_chunks * LANES) == 0:
            return d_ff // n_chunks
    return d_ff


def _oproj_ffn(h, o, wo, gain, wgu, wd):
    n_rows, d = h.shape
    d_ff = wd.shape[0]
    tm = _row_tile(n_rows)
    kern = functools.partial(_oproj_ffn_kernel, d_ff=d_ff, ff_chunk=_ff_chunk(d_ff))
    return pl.pallas_call(
        kern,
        grid=(n_rows // tm,),
        in_specs=[pl.BlockSpec((tm, d), lambda i: (i, 0)),
                  pl.BlockSpec((tm, o.shape[1]), lambda i: (i, 0)),
                  _const_spec(wo.shape),
                  _const_spec((1, d)),
                  _const_spec(wgu.shape),
                  _const_spec(wd.shape)],
        out_specs=pl.BlockSpec((tm, d), lambda i: (i, 0)),
        out_shape=jax.ShapeDtypeStruct((n_rows, d), F32),
        compiler_params=_params(("arbitrary",)),
        name="oproj_ffn",
    )(h, o, wo, gain.reshape(1, d), wgu, wd)


def _tri_ones():
    j = jnp.arange(BLK)[:, None]
    s = jnp.arange(BLK)[None, :]
    tri = (j >= s).astype(BF16)
    return jnp.concatenate([tri, jnp.ones((BLK, BLK), BF16)], axis=1)


def _sb_block(q, kb, vb, bias, tri, r, valid=None):
    z = _dot_nt(q, kb) + bias
    sp = _softplus(z)
    if valid is not None:
        sp = jnp.where(valid, sp, 0.0)
    cr = _dot(sp.astype(BF16), tri)
    a = jnp.exp(z - (cr[:, :BLK] + r))
    if valid is not None:
        a = jnp.where(valid, a, 0.0)
    return _dot(a.astype(BF16), vb), r + cr[:, BLK:]


def _sb_prompt_kernel(q_ref, k_ref, v_ref, bias_ref, tri_ref, o_ref, qs_ref, acc_ref, r_ref):
    i = pl.program_id(2)
    q = q_ref[...]
    for g in range(GROUP):
        qs_ref[g * BLK:(g + 1) * BLK, :] = q[:, g * HEAD_DIM:(g + 1) * HEAD_DIM]
    tri = tri_ref[...]

    def visit(j, diagonal):
        start = pl.multiple_of(j * BLK, BLK)
        kb = k_ref[pl.ds(start, BLK), :]
        vb = v_ref[pl.ds(start, BLK), :]
        valid = None
        if diagonal:
            t = lax.broadcasted_iota(jnp.int32, (BLK, BLK), 0)
            s = lax.broadcasted_iota(jnp.int32, (BLK, BLK), 1)
            valid = s < t
        for g in range(GROUP):
            rows = slice(g * BLK, (g + 1) * BLK)
            r_old = jnp.zeros((BLK, BLK), F32) if diagonal else r_ref[rows, :]
            av, r_new = _sb_block(qs_ref[rows, :], kb, vb, bias_ref[rows, :], tri, r_old, valid)
            r_ref[rows, :] = r_new
            if diagonal:
                acc_ref[rows, :] = av
            else:
                acc_ref[rows, :] += av

    visit(i, True)

    def body(jj, carry):
        visit(i - 1 - jj, False)
        return carry

    lax.fori_loop(0, i, body, 0)
    acc = acc_ref[...]
    o_ref[...] = jnp.concatenate(
        [acc[g * BLK:(g + 1) * BLK, :] for g in range(GROUP)], axis=1).astype(BF16)


def _bias_rows(bias, rows_per_head):
    b = bias.astype(F32).reshape(N_KV, GROUP, 1, 1)
    return jnp.broadcast_to(b, (N_KV, GROUP, rows_per_head, BLK)).reshape(N_KV, GROUP * rows_per_head, BLK)


def _sb_prompt_attn(q, khm, vhm, bias):
    b, s, _ = q.shape
    gw = GROUP * HEAD_DIM
    m = GROUP * BLK
    return pl.pallas_call(
        _sb_prompt_kernel,
        grid=(b, N_KV, s // BLK),
        in_specs=[pl.BlockSpec((None, BLK, gw), lambda bi, h, i: (bi, i, h)),
                  pl.BlockSpec((None, None, s, HEAD_DIM), lambda bi, h, i: (h, bi, 0, 0)),
                  pl.BlockSpec((None, None, s, HEAD_DIM), lambda bi, h, i: (h, bi, 0, 0)),
                  pl.BlockSpec((None, m, BLK), lambda bi, h, i: (h, 0, 0)),
                  _const_spec((BLK, 2 * BLK))],
        out_specs=pl.BlockSpec((None, BLK, gw), lambda bi, h, i: (bi, i, h)),
        out_shape=jax.ShapeDtypeStruct((b, s, Q_WIDTH), BF16),
        scratch_shapes=[pltpu.VMEM((m, HEAD_DIM), BF16),
                        pltpu.VMEM((m, HEAD_DIM), F32),
                        pltpu.VMEM((m, BLK), F32)],
        compiler_params=_params(("arbitrary", "arbitrary", "arbitrary")),
        name="sb_prompt_attn",
    )(q, khm, vhm, _bias_rows(bias, BLK), _tri_ones())


def _block_diag_q(q, n_seq, t):
    qh = q.reshape(n_seq, t, N_KV, GROUP, HEAD_DIM).transpose(0, 2, 3, 1, 4)
    qh = qh.reshape(n_seq, N_KV, GROUP * t, 1, HEAD_DIM)
    sel = (jnp.arange(N_KV)[:, None] == jnp.arange(N_KV)[None, :]).astype(q.dtype)
    out = qh * sel[None, :, None, :, None]
    return out.reshape(n_seq, N_HEADS * t, KV_WIDTH)


def _take_block_diag(acc, rows_per_kv):
    row_h = lax.broadcasted_iota(jnp.int32, (acc.shape[0], HEAD_DIM), 0) // rows_per_kv
    out = acc[:, :HEAD_DIM]
    for h in range(1, N_KV):
        out = jnp.where(row_h == h, acc[:, h * HEAD_DIM:(h + 1) * HEAD_DIM], out)
    return out


def _unstack_heads(o, n_seq, t):
    return o.reshape(n_seq, N_KV, GROUP, t, HEAD_DIM).transpose(0, 3, 1, 2, 4).reshape(n_seq * t, Q_WIDTH)


def _sb_sample_kernel(pt_ref, q_ref, bias_ref, knew_ref, vnew_ref, tri_ref, ck_ref, cv_ref, o_ref,
                      kbuf, vbuf, sem, pad_k, pad_v, acc_ref, r_ref, *, n_pages, t_new):
    b = pl.program_id(0)
    n_chunks = n_pages // PAGES_PER_CHUNK
    rows = q_ref.shape[0]

    def page_copies(chunk, slot, g):
        page = pt_ref[b * n_pages + (n_pages - 1 - (chunk * PAGES_PER_CHUNK + g))]
        return (pltpu.make_async_copy(ck_ref.at[page], kbuf.at[slot, g], sem.at[0, slot]),
                pltpu.make_async_copy(cv_ref.at[page], vbuf.at[slot, g], sem.at[1, slot]))

    def start_chunk(chunk, slot):
        for g in range(PAGES_PER_CHUNK):
            for cp in page_copies(chunk, slot, g):
                cp.start()

    def wait_chunk(chunk, slot):
        for g in range(PAGES_PER_CHUNK):
            for cp in page_copies(chunk, slot, g):
                cp.wait()

    start_chunk(0, 0)

    q = q_ref[...]
    bias = bias_ref[...]
    tri = tri_ref[...]

    pad_k[...] = jnp.zeros(pad_k.shape, F32)
    pad_v[...] = jnp.zeros(pad_v.shape, F32)
    pad_k[0:knew_ref.shape[0], :] = knew_ref[...]
    pad_v[0:vnew_ref.shape[0], :] = vnew_ref[...]
    t_of_row = lax.broadcasted_iota(jnp.int32, (rows, BLK), 0) % t_new
    col = lax.broadcasted_iota(jnp.int32, (rows, BLK), 1)
    av, r0 = _sb_block(q, pad_k[...].astype(BF16), pad_v[...].astype(BF16), bias, tri,
                       jnp.zeros((rows, BLK), F32), col < t_of_row)
    acc_ref[...] = av
    r_ref[...] = r0

    def chunk_body(chunk, carry):
        slot = chunk % 2

        @pl.when(chunk + 1 < n_chunks)
        def _():
            start_chunk(chunk + 1, 1 - slot)

        wait_chunk(chunk, slot)
        for g in range(PAGES_PER_CHUNK):
            av, r_new = _sb_block(q, kbuf[slot, g].astype(BF16), vbuf[slot, g].astype(BF16),
                                  bias, tri, r_ref[...])
            acc_ref[...] += av
            r_ref[...] = r_new
        return carry

    lax.fori_loop(0, n_chunks, chunk_body, 0)
    o_ref[...] = _take_block_diag(acc_ref[...], rows // N_KV).astype(BF16)


def _pad_rows(x, rows):
    return jnp.pad(x, ((0, 0), (0, rows - x.shape[1]), (0, 0)))


def _sb_sample_attn(q, k_new, v_new, bias, cache_k, cache_v, page_table):
    n_seq, n_pages = page_table.shape
    t = q.shape[0] // n_seq
    rows = N_HEADS * t
    qbd = _block_diag_q(q, n_seq, t)
    bias_rows = _bias_rows(bias, t).reshape(rows, BLK)
    new_rows = 8
    knew = _pad_rows(k_new.reshape(n_seq, t, KV_WIDTH), new_rows)
    vnew = _pad_rows(v_new.reshape(n_seq, t, KV_WIDTH), new_rows)
    kern = functools.partial(_sb_sample_kernel, n_pages=n_pages, t_new=t)
    grid_spec = pltpu.PrefetchScalarGridSpec(
        num_scalar_prefetch=1,
        grid=(n_seq,),
        in_specs=[pl.BlockSpec((None, rows, KV_WIDTH), lambda s, pt: (s, 0, 0)),
                  pl.BlockSpec((rows, BLK), lambda s, pt: (0, 0)),
                  pl.BlockSpec((None, new_rows, KV_WIDTH), lambda s, pt: (s, 0, 0)),
                  pl.BlockSpec((None, new_rows, KV_WIDTH), lambda s, pt: (s, 0, 0)),
                  pl.BlockSpec((BLK, 2 * BLK), lambda s, pt: (0, 0)),
                  pl.BlockSpec(memory_space=pl.ANY),
                  pl.BlockSpec(memory_space=pl.ANY)],
        out_specs=pl.BlockSpec((None, rows, HEAD_DIM), lambda s, pt: (s, 0, 0)),
        scratch_shapes=[pltpu.VMEM((2, PAGES_PER_CHUNK, PAGE_SIZE, KV_WIDTH), F32),
                        pltpu.VMEM((2, PAGES_PER_CHUNK, PAGE_SIZE, KV_WIDTH), F32),
                        pltpu.SemaphoreType.DMA((2, 2)),
                        pltpu.VMEM((BLK, KV_WIDTH), F32),
                        pltpu.VMEM((BLK, KV_WIDTH), F32),
                        pltpu.VMEM((rows, KV_WIDTH), F32),
                        pltpu.VMEM((rows, BLK), F32)])
    o = pl.pallas_call(
        kern,
        grid_spec=grid_spec,
        out_shape=jax.ShapeDtypeStruct((n_seq, rows, HEAD_DIM), BF16),
        compiler_params=_params(("arbitrary",)),
        name="sb_sample_attn",
    )(page_table.reshape(-1), qbd, bias_rows, knew, vnew, _tri_ones(), cache_k, cache_v)
    return _unstack_heads(o, n_seq, t)


def _sink_softmax_av(scores, values, sink):
    mx = sink
    for s in scores:
        mx = jnp.maximum(mx, jnp.max(s, axis=-1, keepdims=True))
    ps = [jnp.exp(s - mx) for s in scores]
    den = jnp.exp(sink - mx)
    for p in ps:
        den = den + jnp.sum(p, axis=-1, keepdims=True)
    inv = 1.0 / den
    out = None
    for p, v in zip(ps, values):
        term = _dot((p * inv).astype(BF16), v)
        out = term if out is None else out + term
    return out


def _sw_prompt_kernel(q_ref, kc_ref, kp_ref, vc_ref, vp_ref, sink_ref, o_ref, qs_ref):
    i = pl.program_id(2)
    q = q_ref[...]
    for g in range(GROUP):
        qs_ref[g * BLK:(g + 1) * BLK, :] = q[:, g * HEAD_DIM:(g + 1) * HEAD_DIM]
    qs = qs_ref[...]
    m = GROUP * BLK
    t = lax.broadcasted_iota(jnp.int32, (m, BLK), 0) % BLK
    s = lax.broadcasted_iota(jnp.int32, (m, BLK), 1)
    neg = -jnp.inf
    s_cur = jnp.where(s <= t, _dot_nt(qs, kc_ref[...]), neg)
    first_key = t + jnp.where(i > 0, 0, BLK)
    s_prev = jnp.where(s >= first_key, _dot_nt(qs, kp_ref[...]), neg)
    out = _sink_softmax_av([s_prev, s_cur], [vp_ref[...], vc_ref[...]], sink_ref[...])
    o_ref[...] = jnp.concatenate(
        [out[g * BLK:(g + 1) * BLK, :] for g in range(GROUP)], axis=1).astype(BF16)


def _sink_rows(sinks, rows_per_head):
    sk = sinks.astype(F32).reshape(N_KV, GROUP, 1, 1)
    return jnp.broadcast_to(sk, (N_KV, GROUP, rows_per_head, 1)).reshape(N_KV, GROUP * rows_per_head, 1)


def _sw_prompt_attn(q, khm, vhm, sinks):
    b, s, _ = q.shape
    gw = GROUP * HEAD_DIM
    m = GROUP * BLK
    cur = pl.BlockSpec((None, None, BLK, HEAD_DIM), lambda bi, h, i: (h, bi, i, 0))
    prev = pl.BlockSpec((None, None, BLK, HEAD_DIM), lambda bi, h, i: (h, bi, jnp.maximum(i - 1, 0), 0))
    return pl.pallas_call(
        _sw_prompt_kernel,
        grid=(b, N_KV, s // BLK),
        in_specs=[pl.BlockSpec((None, BLK, gw), lambda bi, h, i: (bi, i, h)),
                  cur, prev, cur, prev,
                  pl.BlockSpec((None, m, 1), lambda bi, h, i: (h, 0, 0))],
        out_specs=pl.BlockSpec((None, BLK, gw), lambda bi, h, i: (bi, i, h)),
        out_shape=jax.ShapeDtypeStruct((b, s, Q_WIDTH), BF16),
        scratch_shapes=[pltpu.VMEM((m, HEAD_DIM), BF16)],
        compiler_params=_params(("arbitrary", "arbitrary", "arbitrary")),
        name="sw_prompt_attn",
    )(q, khm, khm, vhm, vhm, _sink_rows(sinks, BLK))


def _sw_sample_kernel(q_ref, kk_ref, vv_ref, sink_ref, o_ref, *, t_new, n_buf):
    rows = q_ref.shape[1]
    n_keys = kk_ref.shape[1]
    t = lax.broadcasted_iota(jnp.int32, (rows, n_keys), 0) % t_new
    j = lax.broadcasted_iota(jnp.int32, (rows, n_keys), 1)
    valid = jnp.logical_and(j >= n_buf + t - WINDOW, j <= n_buf + t)
    sink = sink_ref[...]
    for i in range(q_ref.shape[0]):
        sc = jnp.where(valid, _dot_nt(q_ref[i], kk_ref[i].astype(BF16)), -jnp.inf)
        out = _sink_softmax_av([sc], [vv_ref[i].astype(BF16)], sink)
        o_ref[i] = _take_block_diag(out, rows // N_KV).astype(BF16)


def _sw_sample_attn(q, kk, vv, sinks, n_buf):
    n_seq = kk.shape[0]
    t = q.shape[0] // n_seq
    rows = N_HEADS * t
    n_keys = -(-kk.shape[1] // LANES) * LANES
    kk = _pad_rows(kk, n_keys)
    vv = _pad_rows(vv, n_keys)
    qbd = _block_diag_q(q, n_seq, t)
    sink_rows = _sink_rows(sinks, t).reshape(rows, 1)
    ns = SEQS_PER_STEP if n_seq % SEQS_PER_STEP == 0 else 1
    kern = functools.partial(_sw_sample_kernel, t_new=t, n_buf=n_buf)
    o = pl.pallas_call(
        kern,
        grid=(n_seq // ns,),
        in_specs=[pl.BlockSpec((ns, rows, KV_WIDTH), lambda s: (s, 0, 0)),
                  pl.BlockSpec((ns, n_keys, KV_WIDTH), lambda s: (s, 0, 0)),
                  pl.BlockSpec((ns, n_keys, KV_WIDTH), lambda s: (s, 0, 0)),
                  pl.BlockSpec((rows, 1), lambda s: (0, 0))],
        out_specs=pl.BlockSpec((ns, rows, HEAD_DIM), lambda s: (s, 0, 0)),
        out_shape=jax.ShapeDtypeStruct((n_seq, rows, HEAD_DIM), BF16),
        compiler_params=_params(("arbitrary",)),
        name="sw_sample_attn",
    )(qbd, kk, vv, sink_rows)
    return _unstack_heads(o, n_seq, t)


def kernel(x_prompt, x_sample, cache_k_sb, cache_v_sb, page_table, state_k_win, state_v_win,
           sb_norm, sb_w_qkv, sb_bias, sb_w_o, sw_norm, sw_w_qkv, sw_q_norm, sw_k_norm, sw_sinks, sw_w_o,
           ffn_norm, ffn_w_gate_up, ffn_w_down):
    b, s, d = x_prompt.shape
    n_seq, t_new, _ = x_sample.shape
    depth = ffn_norm.shape[0]
    past_len = page_table.shape[1] * PAGE_SIZE
    n_buf = state_k_win.shape[2]
    assert s % ROW_TILE == 0 and s % BLK == 0 and page_table.shape[1] % PAGES_PER_CHUNK == 0

    hp = x_prompt.reshape(b * s, d)
    hs = x_sample.reshape(n_seq * t_new, d)
    bf = lambda w: w.astype(BF16)

    prompt_tables = _rope_tables(jnp.arange(s))
    sample_tables = _rope_tables(past_len + jnp.arange(t_new))
    sample_tables = tuple(jnp.tile(tab, (n_seq, 1)) for tab in sample_tables)

    outs = {name: [] for name in ("sb_kp", "sb_vp", "sb_ks", "sb_vs", "sw_kp", "sw_vp", "sw_ks", "sw_vs")}
    for i in range(depth):
        j = i // 2
        if i % 2 == 0:
            w_qkv = bf(sb_w_qkv[j])
            q, k, v, khm, vhm = _sb_qkv(hp, sb_norm[j], w_qkv)
            op = _sb_prompt_attn(q.reshape(b, s, Q_WIDTH), khm.reshape(N_KV, b, s, HEAD_DIM),
                                 vhm.reshape(N_KV, b, s, HEAD_DIM), sb_bias[j]).reshape(b * s, Q_WIDTH)
            qs, ks, vs, _, _ = _sb_qkv(hs, sb_norm[j], w_qkv)
            n_phys = cache_k_sb.shape[1]
            os_ = _sb_sample_attn(qs, ks, vs, sb_bias[j],
                                  cache_k_sb[j].reshape(n_phys, PAGE_SIZE, KV_WIDTH),
                                  cache_v_sb[j].reshape(n_phys, PAGE_SIZE, KV_WIDTH), page_table)
            w_o = bf(sb_w_o[j])
            outs["sb_kp"].append(k.reshape(b, s, N_KV, HEAD_DIM))
            outs["sb_vp"].append(v.reshape(b, s, N_KV, HEAD_DIM))
            outs["sb_ks"].append(ks.reshape(n_seq, t_new, N_KV, HEAD_DIM))
            outs["sb_vs"].append(vs.reshape(n_seq, t_new, N_KV, HEAD_DIM))
        else:
            w_qkv = bf(sw_w_qkv[j])
            q, k, v, khm, vhm = _sw_qkv(hp, sw_norm[j], w_qkv, sw_q_norm[j], sw_k_norm[j],
                                        prompt_tables, s // _row_tile(b * s))
            op = _sw_prompt_attn(q.reshape(b, s, Q_WIDTH), khm.reshape(N_KV, b, s, HEAD_DIM),
                                 vhm.reshape(N_KV, b, s, HEAD_DIM), sw_sinks[j]).reshape(b * s, Q_WIDTH)
            qs, ks, vs, _, _ = _sw_qkv(hs, sw_norm[j], w_qkv, sw_q_norm[j], sw_k_norm[j],
                                       sample_tables, 1)
            kk = jnp.concatenate([state_k_win[j].reshape(n_seq, n_buf, KV_WIDTH),
                                  ks.reshape(n_seq, t_new, KV_WIDTH)], axis=1)
            vv = jnp.concatenate([state_v_win[j].reshape(n_seq, n_buf, KV_WIDTH),
                                  vs.reshape(n_seq, t_new, KV_WIDTH)], axis=1)
            os_ = _sw_sample_attn(qs, kk, vv, sw_sinks[j], n_buf)
            w_o = bf(sw_w_o[j])
            keep = min(WINDOW, s)
            outs["sw_kp"].append(k.reshape(b, s, N_KV, HEAD_DIM)[:, -keep:])
            outs["sw_vp"].append(v.reshape(b, s, N_KV, HEAD_DIM)[:, -keep:])
            outs["sw_ks"].append(kk[:, -n_buf:].reshape(n_seq, n_buf, N_KV, HEAD_DIM))
            outs["sw_vs"].append(vv[:, -n_buf:].reshape(n_seq, n_buf, N_KV, HEAD_DIM))
        w_gu, w_d = bf(ffn_w_gate_up[i]), bf(ffn_w_down[i])
        hp = _oproj_ffn(hp, op, w_o, ffn_norm[i], w_gu, w_d)
        hs = _oproj_ffn(hs, os_, w_o, ffn_norm[i], w_gu, w_d)

    return (hp.reshape(b, s, d), hs.reshape(n_seq, t_new, d),
            jnp.stack(outs["sb_kp"]), jnp.stack(outs["sb_vp"]),
            jnp.stack(outs["sb_ks"]), jnp.stack(outs["sb_vs"]),
            jnp.stack(outs["sw_kp"]), jnp.stack(outs["sw_vp"]),
            jnp.stack(outs["sw_ks"]), jnp.stack(outs["sw_vs"]))
```

```python
import functools

import jax
import jax.numpy as jnp
from jax import lax
from jax.experimental import pallas as pl
from jax.experimental.pallas import tpu as pltpu

F32 = jnp.float32
BF16 = jnp.bfloat16

HEAD_DIM = 64
N_HEADS = 16
N_KV = 4
GROUP = N_HEADS // N_KV
KV_WIDTH = N_KV * HEAD_DIM
Q_WIDTH = N_HEADS * HEAD_DIM
WINDOW = 128
ROT_DIM = HEAD_DIM // 4
ROPE_THETA = 500000.0
EPS = 1e-6
PAGE_SIZE = 128
SCALE = HEAD_DIM ** -0.5
LOG2E = 1.4426950408889634

LANES = 128
BLK = 128
SB_TQ = 512
SB_TK = 256
BIAS_TERMS = 3
ROW_TILE = 512
VMEM_LIMIT = 56 * 1024 * 1024
PAGES_PER_CHUNK = 16
SEQS_PER_STEP = 8


def _params(sem):
    return pltpu.CompilerParams(dimension_semantics=sem, vmem_limit_bytes=VMEM_LIMIT)


def _const_spec(shape):
    nd = len(shape)
    return pl.BlockSpec(shape, lambda *_: (0,) * nd, pipeline_mode=pl.Buffered(1))


def _rms_norm_rows(x, gain):
    ms = jnp.mean(x * x, axis=-1, keepdims=True)
    return x * lax.rsqrt(ms + EPS) * gain


def _softplus(z):
    return jnp.maximum(z, 0.0) + jnp.log(1.0 + jnp.exp2(jnp.abs(z) * (-LOG2E)))


def _dot_nt(a, b):
    return lax.dot_general(a, b, (((1,), (1,)), ((), ())), preferred_element_type=F32)


def _dot(a, b):
    return jnp.dot(a, b, preferred_element_type=F32)


def _store_head_major(dst_ref, x):
    for h in range(N_KV):
        dst_ref[h] = x[:, h * HEAD_DIM:(h + 1) * HEAD_DIM].astype(BF16)


def _sb_qkv_kernel(x_ref, g_ref, w_ref, q_ref, k_ref, v_ref, khm_ref, vhm_ref):
    h = _rms_norm_rows(x_ref[...], g_ref[...]).astype(BF16)
    qkv = _dot(h, w_ref[...])
    q_ref[...] = (qkv[:, :Q_WIDTH] * SCALE).astype(BF16)
    k = qkv[:, Q_WIDTH:Q_WIDTH + KV_WIDTH]
    v = qkv[:, Q_WIDTH + KV_WIDTH:]
    k_ref[...] = k
    v_ref[...] = v
    lane = lax.broadcasted_iota(jnp.int32, (k.shape[0], HEAD_DIM), 1)
    ones_cols = jnp.where(lane < BIAS_TERMS, 1.0, 0.0).astype(BF16)
    for h in range(N_KV):
        khm_ref[h, :, :HEAD_DIM] = k[:, h * HEAD_DIM:(h + 1) * HEAD_DIM].astype(BF16)
        khm_ref[h, :, HEAD_DIM:] = ones_cols
    _store_head_major(vhm_ref, v)


def _split_bf16(x):
    hi = x.astype(BF16)
    lo = (x - hi.astype(F32)).astype(BF16)
    return hi, lo


def _sw_qkv_kernel(x_ref, g_ref, w_ref, gain_ref, psum_ref, pexp_ref, cos_ref, s1_ref, s2_ref,
                   q_ref, k_ref, v_ref, khm_ref, vhm_ref):
    h = _rms_norm_rows(x_ref[...], g_ref[...]).astype(BF16)
    qkv = _dot(h, w_ref[...])
    qk_width = Q_WIDTH + KV_WIDTH
    qk = qkv[:, :qk_width]
    v = qkv[:, qk_width:]
    sq_hi, sq_lo = _split_bf16(qk * qk)
    ss = _dot(sq_hi, psum_ref[...]) + _dot(sq_lo, psum_ref[...])
    inv_hi, inv_lo = _split_bf16(lax.rsqrt(ss * (1.0 / HEAD_DIM) + EPS))
    inv = _dot(inv_hi, pexp_ref[...]) + _dot(inv_lo, pexp_ref[...])
    qkn = qk * inv * gain_ref[...]
    cos, s1, s2 = cos_ref[...], s1_ref[...], s2_ref[...]
    half = ROT_DIM // 2
    for c in range(qk_width // LANES):
        blk = qkn[:, c * LANES:(c + 1) * LANES]
        rot = (blk * cos + pltpu.roll(blk, half, axis=1) * s1
               + pltpu.roll(blk, LANES - half, axis=1) * s2)
        if c < Q_WIDTH // LANES:
            q_ref[:, c * LANES:(c + 1) * LANES] = (rot * SCALE).astype(BF16)
        else:
            j = c - Q_WIDTH // LANES
            k_ref[:, j * LANES:(j + 1) * LANES] = rot
            khm_ref[2 * j] = rot[:, :HEAD_DIM].astype(BF16)
            khm_ref[2 * j + 1] = rot[:, HEAD_DIM:].astype(BF16)
    v_ref[...] = v
    _store_head_major(vhm_ref, v)


def _qkv_out(n_rows, tm, khm_width):
    shapes = (jax.ShapeDtypeStruct((n_rows, Q_WIDTH), BF16),
              jax.ShapeDtypeStruct((n_rows, KV_WIDTH), F32),
              jax.ShapeDtypeStruct((n_rows, KV_WIDTH), F32),
              jax.ShapeDtypeStruct((N_KV, n_rows, khm_width), BF16),
              jax.ShapeDtypeStruct((N_KV, n_rows, HEAD_DIM), BF16))
    specs = (pl.BlockSpec((tm, Q_WIDTH), lambda i: (i, 0)),
             pl.BlockSpec((tm, KV_WIDTH), lambda i: (i, 0)),
             pl.BlockSpec((tm, KV_WIDTH), lambda i: (i, 0)),
             pl.BlockSpec((N_KV, tm, khm_width), lambda i: (0, i, 0)),
             pl.BlockSpec((N_KV, tm, HEAD_DIM), lambda i: (0, i, 0)))
    return shapes, specs


def _row_tile(n_rows):
    return min(ROW_TILE, n_rows)


def _sb_qkv(x, gain, w):
    n_rows, d = x.shape
    tm = _row_tile(n_rows)
    shapes, specs = _qkv_out(n_rows, tm, 2 * HEAD_DIM)
    return pl.pallas_call(
        _sb_qkv_kernel,
        grid=(n_rows // tm,),
        in_specs=[pl.BlockSpec((tm, d), lambda i: (i, 0)),
                  _const_spec((1, d)),
                  _const_spec(w.shape)],
        out_specs=specs, out_shape=shapes,
        compiler_params=_params(("arbitrary",)),
        name="sb_qkv",
    )(x, gain.reshape(1, d), w)


def _rope_tables(pos):
    half = ROT_DIM // 2
    inv_freq = ROPE_THETA ** (-jnp.arange(half, dtype=F32) * 2.0 / ROT_DIM)
    ang = pos.astype(F32)[:, None] * inv_freq[None, :]
    cos, sin = jnp.cos(ang), jnp.sin(ang)
    t = pos.shape[0]
    ones = jnp.ones((t, HEAD_DIM - ROT_DIM), F32)
    zeros_h = jnp.zeros((t, half), F32)
    zeros_r = jnp.zeros((t, HEAD_DIM - ROT_DIM), F32)
    c_head = jnp.concatenate([cos, cos, ones], axis=1)
    s1_head = jnp.concatenate([zeros_h, sin, zeros_r], axis=1)
    s2_head = jnp.concatenate([-sin, zeros_h, zeros_r], axis=1)
    two = lambda a: jnp.concatenate([a, a], axis=1)
    return two(c_head), two(s1_head), two(s2_head)


def _sw_qkv(x, gain, w, q_norm, k_norm, pos_tables, table_blocks):
    n_rows, d = x.shape
    tm = _row_tile(n_rows)
    shapes, specs = _qkv_out(n_rows, tm, HEAD_DIM)
    qk_width = Q_WIDTH + KV_WIDTH
    head_of_lane = jnp.arange(qk_width) // HEAD_DIM
    psum = (head_of_lane[:, None] == jnp.arange(LANES)[None, :]).astype(BF16)
    pexp = psum.T
    gain_qk = jnp.concatenate([jnp.tile(q_norm, N_HEADS), jnp.tile(k_norm, N_KV)]).reshape(1, qk_width)
    tab_spec = pl.BlockSpec((tm, LANES), lambda i: (i % table_blocks, 0))
    return pl.pallas_call(
        _sw_qkv_kernel,
        grid=(n_rows // tm,),
        in_specs=[pl.BlockSpec((tm, d), lambda i: (i, 0)),
                  _const_spec((1, d)),
                  _const_spec(w.shape),
                  _const_spec((1, qk_width)),
                  _const_spec(psum.shape),
                  _const_spec(pexp.shape),
                  tab_spec, tab_spec, tab_spec],
        out_specs=specs, out_shape=shapes,
        compiler_params=_params(("arbitrary",)),
        name="sw_qkv",
    )(x, gain.reshape(1, d), w, gain_qk.astype(F32), psum, pexp, *pos_tables)


def _oproj_ffn_kernel(h_ref, o_ref, wo_ref, g_ref, wgu_ref, wd_ref, out_ref, *, d_ff, ff_chunk):
    h1 = h_ref[...] + _dot(o_ref[...], wo_ref[...])
    n = _rms_norm_rows(h1, g_ref[...]).astype(BF16)
    acc = h1
    for c in range(d_ff // ff_chunk):
        lo = c * ff_chunk
        gate = _dot(n, wgu_ref[:, lo:lo + ff_chunk])
        up = _dot(n, wgu_ref[:, d_ff + lo:d_ff + lo + ff_chunk])
        act = (gate * jax.nn.sigmoid(gate) * up).astype(BF16)
        acc = acc + _dot(act, wd_ref[lo:lo + ff_chunk, :])
    out_ref[...] = acc


def _ff_chunk(d_ff):
    for n_chunks in (2, 1):
        if d_ff % (n_chunks * LANES) == 0:
            return d_ff // n_chunks
    return d_ff


def _oproj_ffn(h, o, wo, gain, wgu, wd):
    n_rows, d = h.shape
    d_ff = wd.shape[0]
    tm = _row_tile(n_rows)
    kern = functools.partial(_oproj_ffn_kernel, d_ff=d_ff, ff_chunk=_ff_chunk(d_ff))
    return pl.pallas_call(
        kern,
        grid=(n_rows // tm,),
        in_specs=[pl.BlockSpec((tm, d), lambda i: (i, 0)),
                  pl.BlockSpec((tm, o.shape[1]), lambda i: (i, 0)),
                  _const_spec(wo.shape),
                  _const_spec((1, d)),
                  _const_spec(wgu.shape),
                  _const_spec(wd.shape)],
        out_specs=pl.BlockSpec((tm, d), lambda i: (i, 0)),
        out_shape=jax.ShapeDtypeStruct((n_rows, d), F32),
        compiler_params=_params(("arbitrary",)),
        name="oproj_ffn",
    )(h, o, wo, gain.reshape(1, d), wgu, wd)


def _sb_prompt_kernel(q_ref, k_ref, v_ref, bcol_ref, tri_ref, o_ref, qs_ref, acc_ref, r_ref):
    i = pl.program_id(2)
    tq = q_ref.shape[0]
    m = GROUP * tq
    q = q_ref[...]
    for g in range(GROUP):
        qs_ref[g * tq:(g + 1) * tq, :HEAD_DIM] = q[:, g * HEAD_DIM:(g + 1) * HEAD_DIM]
    qs_ref[:, HEAD_DIM:] = bcol_ref[...]
    tri = tri_ref[...]

    def visit(j, first, key_offset):
        start = pl.multiple_of(j * SB_TK, SB_TK)
        z = _dot_nt(qs_ref[...], k_ref[pl.ds(start, SB_TK), :])
        sp = _softplus(z)
        if key_offset is not None:
            t = lax.broadcasted_iota(jnp.int32, (m, SB_TK), 0) & (tq - 1)
            s = lax.broadcasted_iota(jnp.int32, (m, SB_TK), 1) + key_offset
            valid = s < t
            sp = jnp.where(valid, sp, 0.0)
        cum = _dot(sp.astype(BF16), tri)
        c = cum if first else cum + r_ref[...]
        a = jnp.exp(z - c)
        if key_offset is not None:
            a = jnp.where(valid, a, 0.0)
        av = _dot(a.astype(BF16), v_ref[pl.ds(start, SB_TK), :])
        if first:
            acc_ref[...] = av
            r_ref[...] = cum[:, 0:1]
        else:
            acc_ref[...] += av
            r_ref[...] += cum[:, 0:1]

    per_q = tq // SB_TK
    for kb in reversed(range(per_q)):
        visit(i * per_q + kb, kb == per_q - 1, kb * SB_TK)

    def body(jj, carry):
        visit(i * per_q - 1 - jj, False, None)
        return carry

    lax.fori_loop(0, i * per_q, body, 0)
    acc = acc_ref[...]
    o_ref[...] = jnp.concatenate(
        [acc[g * tq:(g + 1) * tq, :] for g in range(GROUP)], axis=1).astype(BF16)


def _bias_columns(bias, rows_per_head):
    b = bias.astype(F32)
    hi = b.astype(BF16)
    mid = (b - hi.astype(F32)).astype(BF16)
    lo = (b - hi.astype(F32) - mid.astype(F32)).astype(BF16)
    cols = jnp.stack([hi, mid, lo], axis=-1)
    cols = jnp.pad(cols, ((0, 0), (0, HEAD_DIM - BIAS_TERMS)))
    cols = jnp.broadcast_to(cols.reshape(N_KV, GROUP, 1, HEAD_DIM), (N_KV, GROUP, rows_per_head, HEAD_DIM))
    return cols.reshape(N_KV, GROUP * rows_per_head, HEAD_DIM)


def _tri(n):
    j = jnp.arange(n)[:, None]
    s = jnp.arange(n)[None, :]
    return (j >= s).astype(BF16)


def _bias_rows(bias, rows_per_head):
    b = bias.astype(F32).reshape(N_KV, GROUP, 1, 1)
    return jnp.broadcast_to(b, (N_KV, GROUP, rows_per_head, BLK)).reshape(N_KV, GROUP * rows_per_head, BLK)


def _sb_prompt_attn(q, khm, vhm, bias):
    b, s, _ = q.shape
    gw = GROUP * HEAD_DIM
    tq = SB_TQ
    m = GROUP * tq
    return pl.pallas_call(
        _sb_prompt_kernel,
        grid=(b, N_KV, s // tq),
        in_specs=[pl.BlockSpec((None, tq, gw), lambda bi, h, i: (bi, i, h)),
                  pl.BlockSpec((None, None, s, 2 * HEAD_DIM), lambda bi, h, i: (h, bi, 0, 0)),
                  pl.BlockSpec((None, None, s, HEAD_DIM), lambda bi, h, i: (h, bi, 0, 0)),
                  pl.BlockSpec((None, m, HEAD_DIM), lambda bi, h, i: (h, 0, 0)),
                  _const_spec((SB_TK, SB_TK))],
        out_specs=pl.BlockSpec((None, tq, gw), lambda bi, h, i: (bi, i, h)),
        out_shape=jax.ShapeDtypeStruct((b, s, Q_WIDTH), BF16),
        scratch_shapes=[pltpu.VMEM((m, 2 * HEAD_DIM), BF16),
                        pltpu.VMEM((m, HEAD_DIM), F32),
                        pltpu.VMEM((m, 1), F32)],
        compiler_params=_params(("arbitrary", "arbitrary", "arbitrary")),
        name="sb_prompt_attn",
    )(q, khm, vhm, _bias_columns(bias, tq), _tri(SB_TK))


def _block_diag_q(q, n_seq, t):
    qh = q.reshape(n_seq, t, N_KV, GROUP, HEAD_DIM).transpose(0, 2, 3, 1, 4)
    qh = qh.reshape(n_seq, N_KV, GROUP * t, 1, HEAD_DIM)
    sel = (jnp.arange(N_KV)[:, None] == jnp.arange(N_KV)[None, :]).astype(q.dtype)
    out = qh * sel[None, :, None, :, None]
    return out.reshape(n_seq, N_HEADS * t, KV_WIDTH)


def _take_block_diag(acc, rows_per_kv):
    row_h = lax.broadcasted_iota(jnp.int32, (acc.shape[0], HEAD_DIM), 0) // rows_per_kv
    out = acc[:, :HEAD_DIM]
    for h in range(1, N_KV):
        out = jnp.where(row_h == h, acc[:, h * HEAD_DIM:(h + 1) * HEAD_DIM], out)
    return out


def _unstack_heads(o, n_seq, t):
    return o.reshape(n_seq, N_KV, GROUP, t, HEAD_DIM).transpose(0, 3, 1, 2, 4).reshape(n_seq * t, Q_WIDTH)


def _sb_sample_kernel(pt_ref, q_ref, bias_ref, knew_ref, vnew_ref, tri_ref, ck_ref, cv_ref, o_ref,
                      kbuf, vbuf, sem, *, n_pages, pages_per_chunk, t_new):
    b = pl.program_id(0)
    n_seq = pl.num_programs(0)
    n_chunks = n_pages // pages_per_chunk
    rows = q_ref.shape[0]

    def page_copies(seq, chunk, slot, g):
        page = pt_ref[seq * n_pages + (n_pages - 1 - (chunk * pages_per_chunk + g))]
        return (pltpu.make_async_copy(ck_ref.at[page], kbuf.at[slot, g], sem.at[0, slot]),
                pltpu.make_async_copy(cv_ref.at[page], vbuf.at[slot, g], sem.at[1, slot]))

    def start_chunk(seq, chunk, slot):
        for g in range(pages_per_chunk):
            for cp in page_copies(seq, chunk, slot, g):
                cp.start()

    def wait_chunk(seq, chunk, slot):
        for g in range(pages_per_chunk):
            for cp in page_copies(seq, chunk, slot, g):
                cp.wait()

    def slot_of(chunk):
        return (b * n_chunks + chunk) % 2

    @pl.when(b == 0)
    def _():
        start_chunk(0, 0, 0)

    q = q_ref[...]
    bias = bias_ref[...]
    tri = tri_ref[...]

    def visit(kt_tiles, vt_tiles, r, valid):
        zs = [_dot(q, kt.astype(BF16)) + bias for kt in kt_tiles]
        sps = [_softplus(z) for z in zs]
        if valid is not None:
            sps = [jnp.where(valid, sp, 0.0) for sp in sps]
        cums = [_dot(sp.astype(BF16), tri) for sp in sps]
        av = None
        for z, cum, vt in zip(zs, cums, vt_tiles):
            a = jnp.exp(z - (cum + r))
            if valid is not None:
                a = jnp.where(valid, a, 0.0)
            term = _dot_nt(a.astype(BF16), vt.astype(BF16))
            av = term if av is None else av + term
            r = r + cum[:, 0:1]
        return av, r

    t_of_row = lax.broadcasted_iota(jnp.int32, (rows, BLK), 0) % t_new
    col = lax.broadcasted_iota(jnp.int32, (rows, BLK), 1)
    acc0, r0 = visit([knew_ref[...]], [vnew_ref[...]], jnp.zeros((rows, 1), F32), col < t_of_row)

    def chunk_body(chunk, carry):
        acc, r = carry
        slot = slot_of(chunk)

        @pl.when(chunk + 1 < n_chunks)
        def _():
            start_chunk(b, chunk + 1, 1 - slot)

        @pl.when(jnp.logical_and(chunk + 1 == n_chunks, b + 1 < n_seq))
        def _():
            start_chunk(b + 1, 0, 1 - slot)

        wait_chunk(b, chunk, slot)
        av, r = visit([kbuf[slot, g] for g in range(pages_per_chunk)],
                      [vbuf[slot, g] for g in range(pages_per_chunk)], r, None)
        return acc + av, r

    acc, _ = lax.fori_loop(0, n_chunks, chunk_body, (acc0, r0))
    o_ref[...] = _take_block_diag(acc, rows // N_KV).astype(BF16)


def _pad_rows(x, rows):
    return jnp.pad(x, ((0, 0), (0, rows - x.shape[1]), (0, 0)))


def _transposed_page(x, n_seq, t):
    xt = x.reshape(n_seq, t, KV_WIDTH).transpose(0, 2, 1)
    return jnp.pad(xt, ((0, 0), (0, 0), (0, BLK - t)))


def _sb_sample_attn(q, k_new, v_new, bias, cache_kt, cache_vt, page_table):
    n_seq, n_pages = page_table.shape
    t = q.shape[0] // n_seq
    rows = N_HEADS * t
    qbd = _block_diag_q(q, n_seq, t)
    bias_rows = _bias_rows(bias, t).reshape(rows, BLK)
    ppc = PAGES_PER_CHUNK if n_pages % PAGES_PER_CHUNK == 0 else 1
    kern = functools.partial(_sb_sample_kernel, n_pages=n_pages, pages_per_chunk=ppc, t_new=t)
    grid_spec = pltpu.PrefetchScalarGridSpec(
        num_scalar_prefetch=1,
        grid=(n_seq,),
        in_specs=[pl.BlockSpec((None, rows, KV_WIDTH), lambda s, pt: (s, 0, 0)),
                  pl.BlockSpec((rows, BLK), lambda s, pt: (0, 0)),
                  pl.BlockSpec((None, KV_WIDTH, BLK), lambda s, pt: (s, 0, 0)),
                  pl.BlockSpec((None, KV_WIDTH, BLK), lambda s, pt: (s, 0, 0)),
                  pl.BlockSpec((BLK, BLK), lambda s, pt: (0, 0)),
                  pl.BlockSpec(memory_space=pl.ANY),
                  pl.BlockSpec(memory_space=pl.ANY)],
        out_specs=pl.BlockSpec((None, rows, HEAD_DIM), lambda s, pt: (s, 0, 0)),
        scratch_shapes=[pltpu.VMEM((2, ppc, KV_WIDTH, PAGE_SIZE), F32),
                        pltpu.VMEM((2, ppc, KV_WIDTH, PAGE_SIZE), F32),
                        pltpu.SemaphoreType.DMA((2, 2))])
    o = pl.pallas_call(
        kern,
        grid_spec=grid_spec,
        out_shape=jax.ShapeDtypeStruct((n_seq, rows, HEAD_DIM), BF16),
        compiler_params=_params(("arbitrary",)),
        name="sb_sample_attn",
    )(page_table.reshape(-1), qbd, bias_rows, _transposed_page(k_new, n_seq, t),
      _transposed_page(v_new, n_seq, t), _tri(BLK), cache_kt, cache_vt)
    return _unstack_heads(o, n_seq, t)


def _sink_softmax_av(scores, values, sink):
    mx = sink
    for s in scores:
        mx = jnp.maximum(mx, jnp.max(s, axis=-1, keepdims=True))
    ps = [jnp.exp(s - mx) for s in scores]
    den = jnp.exp(sink - mx)
    for p in ps:
        den = den + jnp.sum(p, axis=-1, keepdims=True)
    inv = 1.0 / den
    out = None
    for p, v in zip(ps, values):
        term = _dot((p * inv).astype(BF16), v)
        out = term if out is None else out + term
    return out


def _sw_prompt_kernel(q_ref, kc_ref, kp_ref, vc_ref, vp_ref, sink_ref, o_ref, qs_ref):
    i = pl.program_id(2)
    q = q_ref[...]
    for g in range(GROUP):
        qs_ref[g * BLK:(g + 1) * BLK, :] = q[:, g * HEAD_DIM:(g + 1) * HEAD_DIM]
    qs = qs_ref[...]
    m = GROUP * BLK
    t = lax.broadcasted_iota(jnp.int32, (m, BLK), 0) % BLK
    s = lax.broadcasted_iota(jnp.int32, (m, BLK), 1)
    neg = -jnp.inf
    s_cur = jnp.where(s <= t, _dot_nt(qs, kc_ref[...]), neg)
    first_key = t + jnp.where(i > 0, 0, BLK)
    s_prev = jnp.where(s >= first_key, _dot_nt(qs, kp_ref[...]), neg)
    out = _sink_softmax_av([s_prev, s_cur], [vp_ref[...], vc_ref[...]], sink_ref[...])
    o_ref[...] = jnp.concatenate(
        [out[g * BLK:(g + 1) * BLK, :] for g in range(GROUP)], axis=1).astype(BF16)


def _sink_rows(sinks, rows_per_head):
    sk = sinks.astype(F32).reshape(N_KV, GROUP, 1, 1)
    return jnp.broadcast_to(sk, (N_KV, GROUP, rows_per_head, 1)).reshape(N_KV, GROUP * rows_per_head, 1)


def _sw_prompt_attn(q, khm, vhm, sinks):
    b, s, _ = q.shape
    gw = GROUP * HEAD_DIM
    m = GROUP * BLK
    cur = pl.BlockSpec((None, None, BLK, HEAD_DIM), lambda bi, h, i: (h, bi, i, 0))
    prev = pl.BlockSpec((None, None, BLK, HEAD_DIM), lambda bi, h, i: (h, bi, jnp.maximum(i - 1, 0), 0))
    return pl.pallas_call(
        _sw_prompt_kernel,
        grid=(b, N_KV, s // BLK),
        in_specs=[pl.BlockSpec((None, BLK, gw), lambda bi, h, i: (bi, i, h)),
                  cur, prev, cur, prev,
                  pl.BlockSpec((None, m, 1), lambda bi, h, i: (h, 0, 0))],
        out_specs=pl.BlockSpec((None, BLK, gw), lambda bi, h, i: (bi, i, h)),
        out_shape=jax.ShapeDtypeStruct((b, s, Q_WIDTH), BF16),
        scratch_shapes=[pltpu.VMEM((m, HEAD_DIM), BF16)],
        compiler_params=_params(("arbitrary", "arbitrary", "arbitrary")),
        name="sw_prompt_attn",
    )(q, khm, khm, vhm, vhm, _sink_rows(sinks, BLK))


def _sw_sample_kernel(q_ref, kk_ref, vv_ref, sink_ref, o_ref, *, t_new, n_buf):
    rows = q_ref.shape[1]
    n_keys = kk_ref.shape[1]
    t = lax.broadcasted_iota(jnp.int32, (rows, n_keys), 0) % t_new
    j = lax.broadcasted_iota(jnp.int32, (rows, n_keys), 1)
    valid = jnp.logical_and(j >= n_buf + t - WINDOW, j <= n_buf + t)
    sink = sink_ref[...]
    for i in range(q_ref.shape[0]):
        sc = jnp.where(valid, _dot_nt(q_ref[i], kk_ref[i].astype(BF16)), -jnp.inf)
        out = _sink_softmax_av([sc], [vv_ref[i].astype(BF16)], sink)
        o_ref[i] = _take_block_diag(out, rows // N_KV).astype(BF16)


def _sw_sample_attn(q, kk, vv, sinks, n_buf):
    n_seq = kk.shape[0]
    t = q.shape[0] // n_seq
    rows = N_HEADS * t
    n_keys = -(-kk.shape[1] // LANES) * LANES
    kk = _pad_rows(kk, n_keys)
    vv = _pad_rows(vv, n_keys)
    qbd = _block_diag_q(q, n_seq, t)
    sink_rows = _sink_rows(sinks, t).reshape(rows, 1)
    ns = SEQS_PER_STEP if n_seq % SEQS_PER_STEP == 0 else 1
    kern = functools.partial(_sw_sample_kernel, t_new=t, n_buf=n_buf)
    o = pl.pallas_call(
        kern,
        grid=(n_seq // ns,),
        in_specs=[pl.BlockSpec((ns, rows, KV_WIDTH), lambda s: (s, 0, 0)),
                  pl.BlockSpec((ns, n_keys, KV_WIDTH), lambda s: (s, 0, 0)),
                  pl.BlockSpec((ns, n_keys, KV_WIDTH), lambda s: (s, 0, 0)),
                  pl.BlockSpec((rows, 1), lambda s: (0, 0))],
        out_specs=pl.BlockSpec((ns, rows, HEAD_DIM), lambda s: (s, 0, 0)),
        out_shape=jax.ShapeDtypeStruct((n_seq, rows, HEAD_DIM), BF16),
        compiler_params=_params(("arbitrary",)),
        name="sw_sample_attn",
    )(qbd, kk, vv, sink_rows)
    return _unstack_heads(o, n_seq, t)


def kernel(x_prompt, x_sample, cache_k_sb, cache_v_sb, page_table, state_k_win, state_v_win,
           sb_norm, sb_w_qkv, sb_bias, sb_w_o, sw_norm, sw_w_qkv, sw_q_norm, sw_k_norm, sw_sinks, sw_w_o,
           ffn_norm, ffn_w_gate_up, ffn_w_down):
    b, s, d = x_prompt.shape
    n_seq, t_new, _ = x_sample.shape
    depth = ffn_norm.shape[0]
    past_len = page_table.shape[1] * PAGE_SIZE
    n_buf = state_k_win.shape[2]
    assert s % ROW_TILE == 0 and s % SB_TQ == 0 and SB_TQ % SB_TK == 0

    hp = x_prompt.reshape(b * s, d)
    hs = x_sample.reshape(n_seq * t_new, d)
    bf = lambda w: w.astype(BF16)

    prompt_tables = _rope_tables(jnp.arange(s))
    sample_tables = _rope_tables(past_len + jnp.arange(t_new))
    sample_tables = tuple(jnp.tile(tab, (n_seq, 1)) for tab in sample_tables)

    outs = {name: [] for name in ("sb_kp", "sb_vp", "sb_ks", "sb_vs", "sw_kp", "sw_vp", "sw_ks", "sw_vs")}
    for i in range(depth):
        j = i // 2
        if i % 2 == 0:
            w_qkv = bf(sb_w_qkv[j])
            q, k, v, khm, vhm = _sb_qkv(hp, sb_norm[j], w_qkv)
            op = _sb_prompt_attn(q.reshape(b, s, Q_WIDTH), khm.reshape(N_KV, b, s, 2 * HEAD_DIM),
                                 vhm.reshape(N_KV, b, s, HEAD_DIM), sb_bias[j]).reshape(b * s, Q_WIDTH)
            qs, ks, vs, _, _ = _sb_qkv(hs, sb_norm[j], w_qkv)
            n_phys = cache_k_sb.shape[1]
            to_pages = lambda c: c.transpose(0, 2, 3, 1).reshape(n_phys, KV_WIDTH, PAGE_SIZE)
            os_ = _sb_sample_attn(qs, ks, vs, sb_bias[j], to_pages(cache_k_sb[j]), to_pages(cache_v_sb[j]),
                                  page_table)
            w_o = bf(sb_w_o[j])
            outs["sb_kp"].append(k.reshape(b, s, N_KV, HEAD_DIM))
            outs["sb_vp"].append(v.reshape(b, s, N_KV, HEAD_DIM))
            outs["sb_ks"].append(ks.reshape(n_seq, t_new, N_KV, HEAD_DIM))
            outs["sb_vs"].append(vs.reshape(n_seq, t_new, N_KV, HEAD_DIM))
        else:
            w_qkv = bf(sw_w_qkv[j])
            q, k, v, khm, vhm = _sw_qkv(hp, sw_norm[j], w_qkv, sw_q_norm[j], sw_k_norm[j],
                                        prompt_tables, s // _row_tile(b * s))
            op = _sw_prompt_attn(q.reshape(b, s, Q_WIDTH), khm.reshape(N_KV, b, s, HEAD_DIM),
                                 vhm.reshape(N_KV, b, s, HEAD_DIM), sw_sinks[j]).reshape(b * s, Q_WIDTH)
            qs, ks, vs, _, _ = _sw_qkv(hs, sw_norm[j], w_qkv, sw_q_norm[j], sw_k_norm[j],
                                       sample_tables, 1)
            kk = jnp.concatenate([state_k_win[j].reshape(n_seq, n_buf, KV_WIDTH),
                                  ks.reshape(n_seq, t_new, KV_WIDTH)], axis=1)
            vv = jnp.concatenate([state_v_win[j].reshape(n_seq, n_buf, KV_WIDTH),
                                  vs.reshape(n_seq, t_new, KV_WIDTH)], axis=1)
            os_ = _sw_sample_attn(qs, kk, vv, sw_sinks[j], n_buf)
            w_o = bf(sw_w_o[j])
            keep = min(WINDOW, s)
            outs["sw_kp"].append(k.reshape(b, s, N_KV, HEAD_DIM)[:, -keep:])
            outs["sw_vp"].append(v.reshape(b, s, N_KV, HEAD_DIM)[:, -keep:])
            outs["sw_ks"].append(kk[:, -n_buf:].reshape(n_seq, n_buf, N_KV, HEAD_DIM))
            outs["sw_vs"].append(vv[:, -n_buf:].reshape(n_seq, n_buf, N_KV, HEAD_DIM))
        w_gu, w_d = bf(ffn_w_gate_up[i]), bf(ffn_w_down[i])
        hp = _oproj_ffn(hp, op, w_o, ffn_norm[i], w_gu, w_d)
        hs = _oproj_ffn(hs, os_, w_o, ffn_norm[i], w_gu, w_d)

    return (hp.reshape(b, s, d), hs.reshape(n_seq, t_new, d),
            jnp.stack(outs["sb_kp"]), jnp.stack(outs["sb_vp"]),
            jnp.stack(outs["sb_ks"]), jnp.stack(outs["sb_vs"]),
            jnp.stack(outs["sw_kp"]), jnp.stack(outs["sw_vp"]),
            jnp.stack(outs["sw_ks"]), jnp.stack(outs["sw_vs"]))
```

```python
import functools

import jax
import jax.numpy as jnp
from jax import lax
from jax.experimental import pallas as pl
from jax.experimental.pallas import tpu as pltpu

F32 = jnp.float32
BF16 = jnp.bfloat16

HEAD_DIM = 64
N_HEADS = 16
N_KV = 4
GROUP = N_HEADS // N_KV
KV_WIDTH = N_KV * HEAD_DIM
Q_WIDTH = N_HEADS * HEAD_DIM
WINDOW = 128
ROT_DIM = HEAD_DIM // 4
ROPE_THETA = 500000.0
EPS = 1e-6
PAGE_SIZE = 128
SCALE = HEAD_DIM ** -0.5
LOG2E = 1.4426950408889634

LANES = 128
BLK = 128
SB_TQ = 512
SB_TK = 256
BIAS_TERMS = 3
ROW_TILE = 512
VMEM_LIMIT = 56 * 1024 * 1024
CHUNK_SLOTS = 3
PAGES_PER_CHUNK = 16
SEQS_PER_STEP = 8


def _params(sem):
    return pltpu.CompilerParams(dimension_semantics=sem, vmem_limit_bytes=VMEM_LIMIT)


def _const_spec(shape):
    nd = len(shape)
    return pl.BlockSpec(shape, lambda *_: (0,) * nd, pipeline_mode=pl.Buffered(1))


def _rms_norm_rows(x, gain):
    ms = jnp.mean(x * x, axis=-1, keepdims=True)
    return x * lax.rsqrt(ms + EPS) * gain


def _softplus(z):
    return jnp.maximum(z, 0.0) + jnp.log(1.0 + jnp.exp2(jnp.abs(z) * (-LOG2E)))


def _dot_nt(a, b):
    return lax.dot_general(a, b, (((1,), (1,)), ((), ())), preferred_element_type=F32)


def _dot(a, b):
    return jnp.dot(a, b, preferred_element_type=F32)


def _store_head_major(dst_ref, x):
    for h in range(N_KV):
        dst_ref[h] = x[:, h * HEAD_DIM:(h + 1) * HEAD_DIM].astype(BF16)


def _sb_qkv_kernel(x_ref, g_ref, w_ref, q_ref, k_ref, v_ref, khm_ref, vhm_ref):
    h = _rms_norm_rows(x_ref[...], g_ref[...]).astype(BF16)
    qkv = _dot(h, w_ref[...])
    q_ref[...] = (qkv[:, :Q_WIDTH] * SCALE).astype(BF16)
    k = qkv[:, Q_WIDTH:Q_WIDTH + KV_WIDTH]
    v = qkv[:, Q_WIDTH + KV_WIDTH:]
    k_ref[...] = k
    v_ref[...] = v
    lane = lax.broadcasted_iota(jnp.int32, (k.shape[0], HEAD_DIM), 1)
    ones_cols = jnp.where(lane < BIAS_TERMS, 1.0, 0.0).astype(BF16)
    for h in range(N_KV):
        khm_ref[h, :, :HEAD_DIM] = k[:, h * HEAD_DIM:(h + 1) * HEAD_DIM].astype(BF16)
        khm_ref[h, :, HEAD_DIM:] = ones_cols
    _store_head_major(vhm_ref, v)


def _split_bf16(x):
    hi = x.astype(BF16)
    lo = (x - hi.astype(F32)).astype(BF16)
    return hi, lo


def _sw_qkv_kernel(x_ref, g_ref, w_ref, gain_ref, psum_ref, pexp_ref, cos_ref, s1_ref, s2_ref,
                   q_ref, k_ref, v_ref, khm_ref, vhm_ref):
    h = _rms_norm_rows(x_ref[...], g_ref[...]).astype(BF16)
    qkv = _dot(h, w_ref[...])
    qk_width = Q_WIDTH + KV_WIDTH
    qk = qkv[:, :qk_width]
    v = qkv[:, qk_width:]
    sq_hi, sq_lo = _split_bf16(qk * qk)
    ss = _dot(sq_hi, psum_ref[...]) + _dot(sq_lo, psum_ref[...])
    inv_hi, inv_lo = _split_bf16(lax.rsqrt(ss * (1.0 / HEAD_DIM) + EPS))
    inv = _dot(inv_hi, pexp_ref[...]) + _dot(inv_lo, pexp_ref[...])
    qkn = qk * inv * gain_ref[...]
    cos, s1, s2 = cos_ref[...], s1_ref[...], s2_ref[...]
    half = ROT_DIM // 2
    for c in range(qk_width // LANES):
        blk = qkn[:, c * LANES:(c + 1) * LANES]
        rot = (blk * cos + pltpu.roll(blk, half, axis=1) * s1
               + pltpu.roll(blk, LANES - half, axis=1) * s2)
        if c < Q_WIDTH // LANES:
            q_ref[:, c * LANES:(c + 1) * LANES] = (rot * SCALE).astype(BF16)
        else:
            j = c - Q_WIDTH // LANES
            k_ref[:, j * LANES:(j + 1) * LANES] = rot
            khm_ref[2 * j] = rot[:, :HEAD_DIM].astype(BF16)
            khm_ref[2 * j + 1] = rot[:, HEAD_DIM:].astype(BF16)
    v_ref[...] = v
    _store_head_major(vhm_ref, v)


def _qkv_out(n_rows, tm, khm_width):
    shapes = (jax.ShapeDtypeStruct((n_rows, Q_WIDTH), BF16),
              jax.ShapeDtypeStruct((n_rows, KV_WIDTH), F32),
              jax.ShapeDtypeStruct((n_rows, KV_WIDTH), F32),
              jax.ShapeDtypeStruct((N_KV, n_rows, khm_width), BF16),
              jax.ShapeDtypeStruct((N_KV, n_rows, HEAD_DIM), BF16))
    specs = (pl.BlockSpec((tm, Q_WIDTH), lambda i: (i, 0)),
             pl.BlockSpec((tm, KV_WIDTH), lambda i: (i, 0)),
             pl.BlockSpec((tm, KV_WIDTH), lambda i: (i, 0)),
             pl.BlockSpec((N_KV, tm, khm_width), lambda i: (0, i, 0)),
             pl.BlockSpec((N_KV, tm, HEAD_DIM), lambda i: (0, i, 0)))
    return shapes, specs


def _row_tile(n_rows):
    return min(ROW_TILE, n_rows)


def _sb_qkv(x, gain, w):
    n_rows, d = x.shape
    tm = _row_tile(n_rows)
    shapes, specs = _qkv_out(n_rows, tm, 2 * HEAD_DIM)
    return pl.pallas_call(
        _sb_qkv_kernel,
        grid=(n_rows // tm,),
        in_specs=[pl.BlockSpec((tm, d), lambda i: (i, 0)),
                  _const_spec((1, d)),
                  _const_spec(w.shape)],
        out_specs=specs, out_shape=shapes,
        compiler_params=_params(("arbitrary",)),
        name="sb_qkv",
    )(x, gain.reshape(1, d), w)


def _rope_tables(pos):
    half = ROT_DIM // 2
    inv_freq = ROPE_THETA ** (-jnp.arange(half, dtype=F32) * 2.0 / ROT_DIM)
    ang = pos.astype(F32)[:, None] * inv_freq[None, :]
    cos, sin = jnp.cos(ang), jnp.sin(ang)
    t = pos.shape[0]
    ones = jnp.ones((t, HEAD_DIM - ROT_DIM), F32)
    zeros_h = jnp.zeros((t, half), F32)
    zeros_r = jnp.zeros((t, HEAD_DIM - ROT_DIM), F32)
    c_head = jnp.concatenate([cos, cos, ones], axis=1)
    s1_head = jnp.concatenate([zeros_h, sin, zeros_r], axis=1)
    s2_head = jnp.concatenate([-sin, zeros_h, zeros_r], axis=1)
    two = lambda a: jnp.concatenate([a, a], axis=1)
    return two(c_head), two(s1_head), two(s2_head)


def _sw_qkv(x, gain, w, q_norm, k_norm, pos_tables, table_blocks):
    n_rows, d = x.shape
    tm = _row_tile(n_rows)
    shapes, specs = _qkv_out(n_rows, tm, HEAD_DIM)
    qk_width = Q_WIDTH + KV_WIDTH
    head_of_lane = jnp.arange(qk_width) // HEAD_DIM
    psum = (head_of_lane[:, None] == jnp.arange(LANES)[None, :]).astype(BF16)
    pexp = psum.T
    gain_qk = jnp.concatenate([jnp.tile(q_norm, N_HEADS), jnp.tile(k_norm, N_KV)]).reshape(1, qk_width)
    tab_spec = pl.BlockSpec((tm, LANES), lambda i: (i % table_blocks, 0))
    return pl.pallas_call(
        _sw_qkv_kernel,
        grid=(n_rows // tm,),
        in_specs=[pl.BlockSpec((tm, d), lambda i: (i, 0)),
                  _const_spec((1, d)),
                  _const_spec(w.shape),
                  _const_spec((1, qk_width)),
                  _const_spec(psum.shape),
                  _const_spec(pexp.shape),
                  tab_spec, tab_spec, tab_spec],
        out_specs=specs, out_shape=shapes,
        compiler_params=_params(("arbitrary",)),
        name="sw_qkv",
    )(x, gain.reshape(1, d), w, gain_qk.astype(F32), psum, pexp, *pos_tables)


def _oproj_ffn_kernel(h_ref, o_ref, wo_ref, g_ref, wgu_ref, wd_ref, out_ref, *, d_ff, ff_chunk):
    h1 = h_ref[...] + _dot(o_ref[...], wo_ref[...])
    n = _rms_norm_rows(h1, g_ref[...]).astype(BF16)
    acc = h1
    for c in range(d_ff // ff_chunk):
        lo = c * ff_chunk
        gate = _dot(n, wgu_ref[:, lo:lo + ff_chunk])
        up = _dot(n, wgu_ref[:, d_ff + lo:d_ff + lo + ff_chunk])
        act = (gate * jax.nn.sigmoid(gate) * up).astype(BF16)
        acc = acc + _dot(act, wd_ref[lo:lo + ff_chunk, :])
    out_ref[...] = acc


def _ff_chunk(d_ff):
    for n_chunks in (2, 1):
        if d_ff % (n_chunks * LANES) == 0:
            return d_ff // n_chunks
    return d_ff


def _oproj_ffn(h, o, wo, gain, wgu, wd):
    n_rows, d = h.shape
    d_ff = wd.shape[0]
    tm = _row_tile(n_rows)
    kern = functools.partial(_oproj_ffn_kernel, d_ff=d_ff, ff_chunk=_ff_chunk(d_ff))
    return pl.pallas_call(
        kern,
        grid=(n_rows // tm,),
        in_specs=[pl.BlockSpec((tm, d), lambda i: (i, 0)),
                  pl.BlockSpec((tm, o.shape[1]), lambda i: (i, 0)),
                  _const_spec(wo.shape),
                  _const_spec((1, d)),
                  _const_spec(wgu.shape),
                  _const_spec(wd.shape)],
        out_specs=pl.BlockSpec((tm, d), lambda i: (i, 0)),
        out_shape=jax.ShapeDtypeStruct((n_rows, d), F32),
        compiler_params=_params(("arbitrary",)),
        name="oproj_ffn",
    )(h, o, wo, gain.reshape(1, d), wgu, wd)


def _sb_prompt_kernel(q_ref, k_ref, v_ref, bcol_ref, tri_ref, o_ref, qs_ref, acc_ref, r_ref,
                      z_buf, s_buf, a_buf):
    i = pl.program_id(2)
    tq = q_ref.shape[0]
    m = GROUP * tq
    q = q_ref[...]
    for g in range(GROUP):
        qs_ref[g * tq:(g + 1) * tq, :HEAD_DIM] = q[:, g * HEAD_DIM:(g + 1) * HEAD_DIM]
    qs_ref[:, HEAD_DIM:] = bcol_ref[...]
    tri = tri_ref[...]

    def visit(j, first, key_offset):
        start = pl.multiple_of(j * SB_TK, SB_TK)
        z = _dot_nt(qs_ref[...], k_ref[pl.ds(start, SB_TK), :])
        sp = _softplus(z)
        if key_offset is not None:
            t = lax.broadcasted_iota(jnp.int32, (m, SB_TK), 0) & (tq - 1)
            s = lax.broadcasted_iota(jnp.int32, (m, SB_TK), 1) + key_offset
            valid = s < t
            sp = jnp.where(valid, sp, 0.0)
        cum = _dot(sp.astype(BF16), tri)
        c = cum if first else cum + r_ref[...]
        a = jnp.exp(z - c)
        if key_offset is not None:
            a = jnp.where(valid, a, 0.0)
        av = _dot(a.astype(BF16), v_ref[pl.ds(start, SB_TK), :])
        if first:
            acc_ref[...] = av
            r_ref[...] = cum[:, 0:1]
        else:
            acc_ref[...] += av
            r_ref[...] += cum[:, 0:1]

    per_q = tq // SB_TK
    for kb in reversed(range(per_q)):
        visit(i * per_q + kb, kb == per_q - 1, kb * SB_TK)

    def key_rows(j):
        return pl.ds(pl.multiple_of(j * SB_TK, SB_TK), SB_TK)

    def stage_a(j, slot):
        z = _dot_nt(qs_ref[...], k_ref[key_rows(j), :])
        z_buf[slot] = z
        s_buf[slot] = _softplus(z).astype(BF16)

    def stage_b(slot):
        cum = _dot(s_buf[slot], tri)
        a_buf[slot] = jnp.exp(z_buf[slot] - (cum + r_ref[...])).astype(BF16)
        r_ref[...] += cum[:, 0:1]

    def stage_c(j, slot):
        acc_ref[...] += _dot(a_buf[slot], v_ref[key_rows(j), :])

    @pl.when(i > 0)
    def _():
        n_full = i * per_q
        blk = lambda tau: n_full - 1 - tau
        stage_a(blk(0), 0)
        stage_a(blk(1), 1)
        stage_b(0)

        def body(d, carry):
            tau = 2 + 2 * d
            stage_c(blk(tau - 2), 0)
            stage_b(1)
            stage_a(blk(tau), 0)
            stage_c(blk(tau - 1), 1)
            stage_b(0)
            stage_a(blk(tau + 1), 1)
            return carry

        lax.fori_loop(0, n_full // 2 - 1, body, 0)
        stage_c(blk(n_full - 2), 0)
        stage_b(1)
        stage_c(blk(n_full - 1), 1)

    acc = acc_ref[...]
    o_ref[...] = jnp.concatenate(
        [acc[g * tq:(g + 1) * tq, :] for g in range(GROUP)], axis=1).astype(BF16)


def _bias_columns(bias, rows_per_head):
    b = bias.astype(F32)
    hi = b.astype(BF16)
    mid = (b - hi.astype(F32)).astype(BF16)
    lo = (b - hi.astype(F32) - mid.astype(F32)).astype(BF16)
    cols = jnp.stack([hi, mid, lo], axis=-1)
    cols = jnp.pad(cols, ((0, 0), (0, HEAD_DIM - BIAS_TERMS)))
    cols = jnp.broadcast_to(cols.reshape(N_KV, GROUP, 1, HEAD_DIM), (N_KV, GROUP, rows_per_head, HEAD_DIM))
    return cols.reshape(N_KV, GROUP * rows_per_head, HEAD_DIM)


def _tri(n):
    j = jnp.arange(n)[:, None]
    s = jnp.arange(n)[None, :]
    return (j >= s).astype(BF16)


def _bias_rows(bias, rows_per_head):
    b = bias.astype(F32).reshape(N_KV, GROUP, 1, 1)
    return jnp.broadcast_to(b, (N_KV, GROUP, rows_per_head, BLK)).reshape(N_KV, GROUP * rows_per_head, BLK)


def _sb_prompt_attn(q, khm, vhm, bias):
    b, s, _ = q.shape
    gw = GROUP * HEAD_DIM
    tq = SB_TQ
    m = GROUP * tq
    return pl.pallas_call(
        _sb_prompt_kernel,
        grid=(b, N_KV, s // tq),
        in_specs=[pl.BlockSpec((None, tq, gw), lambda bi, h, i: (bi, i, h)),
                  pl.BlockSpec((None, None, s, 2 * HEAD_DIM), lambda bi, h, i: (h, bi, 0, 0)),
                  pl.BlockSpec((None, None, s, HEAD_DIM), lambda bi, h, i: (h, bi, 0, 0)),
                  pl.BlockSpec((None, m, HEAD_DIM), lambda bi, h, i: (h, 0, 0)),
                  _const_spec((SB_TK, SB_TK))],
        out_specs=pl.BlockSpec((None, tq, gw), lambda bi, h, i: (bi, i, h)),
        out_shape=jax.ShapeDtypeStruct((b, s, Q_WIDTH), BF16),
        scratch_shapes=[pltpu.VMEM((m, 2 * HEAD_DIM), BF16),
                        pltpu.VMEM((m, HEAD_DIM), F32),
                        pltpu.VMEM((m, 1), F32),
                        pltpu.VMEM((2, m, SB_TK), F32),
                        pltpu.VMEM((2, m, SB_TK), BF16),
                        pltpu.VMEM((2, m, SB_TK), BF16)],
        compiler_params=_params(("arbitrary", "arbitrary", "arbitrary")),
        name="sb_prompt_attn",
    )(q, khm, vhm, _bias_columns(bias, tq), _tri(SB_TK))


def _block_diag_q(q, n_seq, t):
    qh = q.reshape(n_seq, t, N_KV, GROUP, HEAD_DIM).transpose(0, 2, 3, 1, 4)
    qh = qh.reshape(n_seq, N_KV, GROUP * t, 1, HEAD_DIM)
    sel = (jnp.arange(N_KV)[:, None] == jnp.arange(N_KV)[None, :]).astype(q.dtype)
    out = qh * sel[None, :, None, :, None]
    return out.reshape(n_seq, N_HEADS * t, KV_WIDTH)


def _take_block_diag(acc, rows_per_kv):
    row_h = lax.broadcasted_iota(jnp.int32, (acc.shape[0], HEAD_DIM), 0) // rows_per_kv
    out = acc[:, :HEAD_DIM]
    for h in range(1, N_KV):
        out = jnp.where(row_h == h, acc[:, h * HEAD_DIM:(h + 1) * HEAD_DIM], out)
    return out


def _unstack_heads(o, n_seq, t):
    return o.reshape(n_seq, N_KV, GROUP, t, HEAD_DIM).transpose(0, 3, 1, 2, 4).reshape(n_seq * t, Q_WIDTH)


def _sb_sample_kernel(pt_ref, q_ref, bias_ref, knew_ref, vnew_ref, tri_ref, ck_ref, cv_ref, o_ref,
                      kbuf, vbuf, sem, *, n_seq, n_pages, pages_per_chunk, t_new):
    b = pl.program_id(0)
    n_chunks = n_pages // pages_per_chunk
    rows = q_ref.shape[0]

    total_chunks = n_seq * n_chunks
    lookahead = CHUNK_SLOTS - 1

    def page_copies(number, g):
        seq = lax.div(number, n_chunks)
        chunk = lax.rem(number, n_chunks)
        slot = lax.rem(number, CHUNK_SLOTS)
        page = pt_ref[seq * n_pages + (n_pages - 1 - (chunk * pages_per_chunk + g))]
        return (pltpu.make_async_copy(ck_ref.at[page], kbuf.at[slot, g], sem.at[0, slot]),
                pltpu.make_async_copy(cv_ref.at[page], vbuf.at[slot, g], sem.at[1, slot]))

    def start_chunk(number):
        for g in range(pages_per_chunk):
            k_copy, v_copy = page_copies(number, g)
            k_copy.start()
            v_copy.start(priority=1)

    def wait_chunk(number):
        for g in range(pages_per_chunk):
            for cp in page_copies(number, g):
                cp.wait()

    @pl.when(b == 0)
    def _():
        for number in range(min(lookahead, total_chunks)):
            start_chunk(jnp.int32(number))

    q = q_ref[...]
    bias = bias_ref[...]
    tri = tri_ref[...]

    def visit(kt_tiles, vt_tiles, r, valid):
        zs = [_dot(q, kt.astype(BF16)) + bias for kt in kt_tiles]
        sps = [_softplus(z) for z in zs]
        if valid is not None:
            sps = [jnp.where(valid, sp, 0.0) for sp in sps]
        cums = [_dot(sp.astype(BF16), tri) for sp in sps]
        av = None
        for z, cum, vt in zip(zs, cums, vt_tiles):
            a = jnp.exp(z - (cum + r))
            if valid is not None:
                a = jnp.where(valid, a, 0.0)
            term = _dot_nt(a.astype(BF16), vt.astype(BF16))
            av = term if av is None else av + term
            r = r + cum[:, 0:1]
        return av, r

    t_of_row = lax.broadcasted_iota(jnp.int32, (rows, BLK), 0) % t_new
    col = lax.broadcasted_iota(jnp.int32, (rows, BLK), 1)
    acc0, r0 = visit([knew_ref[...]], [vnew_ref[...]], jnp.zeros((rows, 1), F32), col < t_of_row)

    def chunk_body(chunk, carry):
        acc, r = carry
        number = b * n_chunks + chunk
        slot = lax.rem(number, CHUNK_SLOTS)

        @pl.when(number + lookahead < total_chunks)
        def _():
            start_chunk(number + lookahead)

        wait_chunk(number)
        av, r = visit([kbuf[slot, g] for g in range(pages_per_chunk)],
                      [vbuf[slot, g] for g in range(pages_per_chunk)], r, None)
        return acc + av, r

    acc, _ = lax.fori_loop(0, n_chunks, chunk_body, (acc0, r0))
    o_ref[...] = _take_block_diag(acc, rows // N_KV).astype(BF16)


def _pad_rows(x, rows):
    return jnp.pad(x, ((0, 0), (0, rows - x.shape[1]), (0, 0)))


def _transposed_page(x, n_seq, t):
    xt = x.reshape(n_seq, t, KV_WIDTH).transpose(0, 2, 1)
    return jnp.pad(xt, ((0, 0), (0, 0), (0, BLK - t)))


def _sb_sample_attn(q, k_new, v_new, bias, cache_kt, cache_vt, page_table):
    n_seq, n_pages = page_table.shape
    t = q.shape[0] // n_seq
    rows = N_HEADS * t
    qbd = _block_diag_q(q, n_seq, t)
    bias_rows = _bias_rows(bias, t).reshape(rows, BLK)
    ppc = PAGES_PER_CHUNK if n_pages % PAGES_PER_CHUNK == 0 else 1
    kern = functools.partial(_sb_sample_kernel, n_seq=n_seq, n_pages=n_pages, pages_per_chunk=ppc, t_new=t)
    grid_spec = pltpu.PrefetchScalarGridSpec(
        num_scalar_prefetch=1,
        grid=(n_seq,),
        in_specs=[pl.BlockSpec((None, rows, KV_WIDTH), lambda s, pt: (s, 0, 0)),
                  pl.BlockSpec((rows, BLK), lambda s, pt: (0, 0)),
                  pl.BlockSpec((None, KV_WIDTH, BLK), lambda s, pt: (s, 0, 0)),
                  pl.BlockSpec((None, KV_WIDTH, BLK), lambda s, pt: (s, 0, 0)),
                  pl.BlockSpec((BLK, BLK), lambda s, pt: (0, 0)),
                  pl.BlockSpec(memory_space=pl.ANY),
                  pl.BlockSpec(memory_space=pl.ANY)],
        out_specs=pl.BlockSpec((None, rows, HEAD_DIM), lambda s, pt: (s, 0, 0)),
        scratch_shapes=[pltpu.VMEM((CHUNK_SLOTS, ppc, KV_WIDTH, PAGE_SIZE), F32),
                        pltpu.VMEM((CHUNK_SLOTS, ppc, KV_WIDTH, PAGE_SIZE), F32),
                        pltpu.SemaphoreType.DMA((2, CHUNK_SLOTS))])
    o = pl.pallas_call(
        kern,
        grid_spec=grid_spec,
        out_shape=jax.ShapeDtypeStruct((n_seq, rows, HEAD_DIM), BF16),
        compiler_params=_params(("arbitrary",)),
        name="sb_sample_attn",
    )(page_table.reshape(-1), qbd, bias_rows, _transposed_page(k_new, n_seq, t),
      _transposed_page(v_new, n_seq, t), _tri(BLK), cache_kt, cache_vt)
    return _unstack_heads(o, n_seq, t)


def _sink_softmax_av(scores, values, sink):
    mx = sink
    for s in scores:
        mx = jnp.maximum(mx, jnp.max(s, axis=-1, keepdims=True))
    ps = [jnp.exp(s - mx) for s in scores]
    den = jnp.exp(sink - mx)
    for p in ps:
        den = den + jnp.sum(p, axis=-1, keepdims=True)
    inv = 1.0 / den
    out = None
    for p, v in zip(ps, values):
        term = _dot((p * inv).astype(BF16), v)
        out = term if out is None else out + term
    return out


def _sw_prompt_kernel(q_ref, kc_ref, kp_ref, vc_ref, vp_ref, sink_ref, o_ref, qs_ref):
    i = pl.program_id(2)
    q = q_ref[...]
    for g in range(GROUP):
        qs_ref[g * BLK:(g + 1) * BLK, :] = q[:, g * HEAD_DIM:(g + 1) * HEAD_DIM]
    qs = qs_ref[...]
    m = GROUP * BLK
    t = lax.broadcasted_iota(jnp.int32, (m, BLK), 0) % BLK
    s = lax.broadcasted_iota(jnp.int32, (m, BLK), 1)
    neg = -jnp.inf
    s_cur = jnp.where(s <= t, _dot_nt(qs, kc_ref[...]), neg)
    first_key = t + jnp.where(i > 0, 0, BLK)
    s_prev = jnp.where(s >= first_key, _dot_nt(qs, kp_ref[...]), neg)
    out = _sink_softmax_av([s_prev, s_cur], [vp_ref[...], vc_ref[...]], sink_ref[...])
    o_ref[...] = jnp.concatenate(
        [out[g * BLK:(g + 1) * BLK, :] for g in range(GROUP)], axis=1).astype(BF16)


def _sink_rows(sinks, rows_per_head):
    sk = sinks.astype(F32).reshape(N_KV, GROUP, 1, 1)
    return jnp.broadcast_to(sk, (N_KV, GROUP, rows_per_head, 1)).reshape(N_KV, GROUP * rows_per_head, 1)


def _sw_prompt_attn(q, khm, vhm, sinks):
    b, s, _ = q.shape
    gw = GROUP * HEAD_DIM
    m = GROUP * BLK
    cur = pl.BlockSpec((None, None, BLK, HEAD_DIM), lambda bi, h, i: (h, bi, i, 0))
    prev = pl.BlockSpec((None, None, BLK, HEAD_DIM), lambda bi, h, i: (h, bi, jnp.maximum(i - 1, 0), 0))
    return pl.pallas_call(
        _sw_prompt_kernel,
        grid=(b, N_KV, s // BLK),
        in_specs=[pl.BlockSpec((None, BLK, gw), lambda bi, h, i: (bi, i, h)),
                  cur, prev, cur, prev,
                  pl.BlockSpec((None, m, 1), lambda bi, h, i: (h, 0, 0))],
        out_specs=pl.BlockSpec((None, BLK, gw), lambda bi, h, i: (bi, i, h)),
        out_shape=jax.ShapeDtypeStruct((b, s, Q_WIDTH), BF16),
        scratch_shapes=[pltpu.VMEM((m, HEAD_DIM), BF16)],
        compiler_params=_params(("arbitrary", "arbitrary", "arbitrary")),
        name="sw_prompt_attn",
    )(q, khm, khm, vhm, vhm, _sink_rows(sinks, BLK))


def _sw_sample_kernel(q_ref, kk_ref, vv_ref, sink_ref, o_ref, *, t_new, n_buf):
    rows = q_ref.shape[1]
    n_keys = kk_ref.shape[1]
    t = lax.broadcasted_iota(jnp.int32, (rows, n_keys), 0) % t_new
    j = lax.broadcasted_iota(jnp.int32, (rows, n_keys), 1)
    valid = jnp.logical_and(j >= n_buf + t - WINDOW, j <= n_buf + t)
    sink = sink_ref[...]
    for i in range(q_ref.shape[0]):
        sc = jnp.where(valid, _dot_nt(q_ref[i], kk_ref[i].astype(BF16)), -jnp.inf)
        out = _sink_softmax_av([sc], [vv_ref[i].astype(BF16)], sink)
        o_ref[i] = _take_block_diag(out, rows // N_KV).astype(BF16)


def _sw_sample_attn(q, kk, vv, sinks, n_buf):
    n_seq = kk.shape[0]
    t = q.shape[0] // n_seq
    rows = N_HEADS * t
    n_keys = -(-kk.shape[1] // LANES) * LANES
    kk = _pad_rows(kk, n_keys)
    vv = _pad_rows(vv, n_keys)
    qbd = _block_diag_q(q, n_seq, t)
    sink_rows = _sink_rows(sinks, t).reshape(rows, 1)
    ns = SEQS_PER_STEP if n_seq % SEQS_PER_STEP == 0 else 1
    kern = functools.partial(_sw_sample_kernel, t_new=t, n_buf=n_buf)
    o = pl.pallas_call(
        kern,
        grid=(n_seq // ns,),
        in_specs=[pl.BlockSpec((ns, rows, KV_WIDTH), lambda s: (s, 0, 0)),
                  pl.BlockSpec((ns, n_keys, KV_WIDTH), lambda s: (s, 0, 0)),
                  pl.BlockSpec((ns, n_keys, KV_WIDTH), lambda s: (s, 0, 0)),
                  pl.BlockSpec((rows, 1), lambda s: (0, 0))],
        out_specs=pl.BlockSpec((ns, rows, HEAD_DIM), lambda s: (s, 0, 0)),
        out_shape=jax.ShapeDtypeStruct((n_seq, rows, HEAD_DIM), BF16),
        compiler_params=_params(("arbitrary",)),
        name="sw_sample_attn",
    )(qbd, kk, vv, sink_rows)
    return _unstack_heads(o, n_seq, t)


def kernel(x_prompt, x_sample, cache_k_sb, cache_v_sb, page_table, state_k_win, state_v_win,
           sb_norm, sb_w_qkv, sb_bias, sb_w_o, sw_norm, sw_w_qkv, sw_q_norm, sw_k_norm, sw_sinks, sw_w_o,
           ffn_norm, ffn_w_gate_up, ffn_w_down):
    b, s, d = x_prompt.shape
    n_seq, t_new, _ = x_sample.shape
    depth = ffn_norm.shape[0]
    past_len = page_table.shape[1] * PAGE_SIZE
    n_buf = state_k_win.shape[2]
    assert s % ROW_TILE == 0 and s % SB_TQ == 0 and SB_TQ == 2 * SB_TK

    hp = x_prompt.reshape(b * s, d)
    hs = x_sample.reshape(n_seq * t_new, d)
    bf = lambda w: w.astype(BF16)

    prompt_tables = _rope_tables(jnp.arange(s))
    sample_tables = _rope_tables(past_len + jnp.arange(t_new))
    sample_tables = tuple(jnp.tile(tab, (n_seq, 1)) for tab in sample_tables)

    outs = {name: [] for name in ("sb_kp", "sb_vp", "sb_ks", "sb_vs", "sw_kp", "sw_vp", "sw_ks", "sw_vs")}
    for i in range(depth):
        j = i // 2
        if i % 2 == 0:
            w_qkv = bf(sb_w_qkv[j])
            q, k, v, khm, vhm = _sb_qkv(hp, sb_norm[j], w_qkv)
            op = _sb_prompt_attn(q.reshape(b, s, Q_WIDTH), khm.reshape(N_KV, b, s, 2 * HEAD_DIM),
                                 vhm.reshape(N_KV, b, s, HEAD_DIM), sb_bias[j]).reshape(b * s, Q_WIDTH)
            qs, ks, vs, _, _ = _sb_qkv(hs, sb_norm[j], w_qkv)
            n_phys = cache_k_sb.shape[1]
            to_pages = lambda c: c.transpose(0, 2, 3, 1).reshape(n_phys, KV_WIDTH, PAGE_SIZE)
            os_ = _sb_sample_attn(qs, ks, vs, sb_bias[j], to_pages(cache_k_sb[j]), to_pages(cache_v_sb[j]),
                                  page_table)
            w_o = bf(sb_w_o[j])
            outs["sb_kp"].append(k.reshape(b, s, N_KV, HEAD_DIM))
            outs["sb_vp"].append(v.reshape(b, s, N_KV, HEAD_DIM))
            outs["sb_ks"].append(ks.reshape(n_seq, t_new, N_KV, HEAD_DIM))
            outs["sb_vs"].append(vs.reshape(n_seq, t_new, N_KV, HEAD_DIM))
        else:
            w_qkv = bf(sw_w_qkv[j])
            q, k, v, khm, vhm = _sw_qkv(hp, sw_norm[j], w_qkv, sw_q_norm[j], sw_k_norm[j],
                                        prompt_tables, s // _row_tile(b * s))
            op = _sw_prompt_attn(q.reshape(b, s, Q_WIDTH), khm.reshape(N_KV, b, s, HEAD_DIM),
                                 vhm.reshape(N_KV, b, s, HEAD_DIM), sw_sinks[j]).reshape(b * s, Q_WIDTH)
            qs, ks, vs, _, _ = _sw_qkv(hs, sw_norm[j], w_qkv, sw_q_norm[j], sw_k_norm[j],
                                       sample_tables, 1)
            kk = jnp.concatenate([state_k_win[j].reshape(n_seq, n_buf, KV_WIDTH),
                                  ks.reshape(n_seq, t_new, KV_WIDTH)], axis=1)
            vv = jnp.concatenate([state_v_win[j].reshape(n_seq, n_buf, KV_WIDTH),
                                  vs.reshape(n_seq, t_new, KV_WIDTH)], axis=1)
            os_ = _sw_sample_attn(qs, kk, vv, sw_sinks[j], n_buf)
            w_o = bf(sw_w_o[j])
            keep = min(WINDOW, s)
            last = lambda x: x.reshape(b, s, KV_WIDTH)[:, s - keep:, :].reshape(b, keep, N_KV, HEAD_DIM)
            outs["sw_kp"].append(last(k))
            outs["sw_vp"].append(last(v))
            outs["sw_ks"].append(kk[:, -n_buf:].reshape(n_seq, n_buf, N_KV, HEAD_DIM))
            outs["sw_vs"].append(vv[:, -n_buf:].reshape(n_seq, n_buf, N_KV, HEAD_DIM))
        w_gu, w_d = bf(ffn_w_gate_up[i]), bf(ffn_w_down[i])
        hp = _oproj_ffn(hp, op, w_o, ffn_norm[i], w_gu, w_d)
        hs = _oproj_ffn(hs, os_, w_o, ffn_norm[i], w_gu, w_d)

    return (hp.reshape(b, s, d), hs.reshape(n_seq, t_new, d),
            jnp.stack(outs["sb_kp"]), jnp.stack(outs["sb_vp"]),
            jnp.stack(outs["sb_ks"]), jnp.stack(outs["sb_vs"]),
            jnp.stack(outs["sw_kp"]), jnp.stack(outs["sw_vp"]),
            jnp.stack(outs["sw_ks"]), jnp.stack(outs["sw_vs"]))
```

```python
import functools

import jax
import jax.numpy as jnp
from jax import lax
from jax.experimental import pallas as pl
from jax.experimental.pallas import tpu as pltpu

F32 = jnp.float32
BF16 = jnp.bfloat16

HEAD_DIM = 64
N_HEADS = 16
N_KV = 4
GROUP = N_HEADS // N_KV
KV_WIDTH = N_KV * HEAD_DIM
Q_WIDTH = N_HEADS * HEAD_DIM
WINDOW = 128
ROT_DIM = HEAD_DIM // 4
ROPE_THETA = 500000.0
EPS = 1e-6
PAGE_SIZE = 128
SCALE = HEAD_DIM ** -0.5
LOG2E = 1.4426950408889634

LANES = 128
BLK = 128
SB_TQ = 512
SB_TK = 256
BIAS_TERMS = 3
ROW_TILE = 512
VMEM_LIMIT = 56 * 1024 * 1024
CHUNK_SLOTS = 3
PAGES_PER_CHUNK = 16
SEQS_PER_STEP = 8


def _params(sem):
    return pltpu.CompilerParams(dimension_semantics=sem, vmem_limit_bytes=VMEM_LIMIT)


def _const_spec(shape):
    nd = len(shape)
    return pl.BlockSpec(shape, lambda *_: (0,) * nd, pipeline_mode=pl.Buffered(1))


def _rms_norm_rows(x, gain):
    ms = jnp.mean(x * x, axis=-1, keepdims=True)
    return x * lax.rsqrt(ms + EPS) * gain


def _softplus(z):
    return jnp.maximum(z, 0.0) + jnp.log(1.0 + jnp.exp2(jnp.abs(z) * (-LOG2E)))


def _dot_nt(a, b):
    return lax.dot_general(a, b, (((1,), (1,)), ((), ())), preferred_element_type=F32)


def _dot(a, b):
    return jnp.dot(a, b, preferred_element_type=F32)


def _store_head_major(dst_ref, x):
    for h in range(N_KV):
        dst_ref[h] = x[:, h * HEAD_DIM:(h + 1) * HEAD_DIM].astype(BF16)


def _sb_qkv_kernel(x_ref, g_ref, w_ref, q_ref, k_ref, v_ref, khm_ref, vhm_ref):
    h = _rms_norm_rows(x_ref[...], g_ref[...]).astype(BF16)
    qkv = _dot(h, w_ref[...])
    q_ref[...] = (qkv[:, :Q_WIDTH] * SCALE).astype(BF16)
    k = qkv[:, Q_WIDTH:Q_WIDTH + KV_WIDTH]
    v = qkv[:, Q_WIDTH + KV_WIDTH:]
    k_ref[...] = k
    v_ref[...] = v
    lane = lax.broadcasted_iota(jnp.int32, (k.shape[0], HEAD_DIM), 1)
    ones_cols = jnp.where(lane < BIAS_TERMS, 1.0, 0.0).astype(BF16)
    for h in range(N_KV):
        khm_ref[h, :, :HEAD_DIM] = k[:, h * HEAD_DIM:(h + 1) * HEAD_DIM].astype(BF16)
        khm_ref[h, :, HEAD_DIM:] = ones_cols
    _store_head_major(vhm_ref, v)


def _split_bf16(x):
    hi = x.astype(BF16)
    lo = (x - hi.astype(F32)).astype(BF16)
    return hi, lo


def _sw_qkv_kernel(x_ref, g_ref, w_ref, gain_ref, psum_ref, pexp_ref, cos_ref, s1_ref, s2_ref,
                   q_ref, k_ref, v_ref, khm_ref, vhm_ref):
    h = _rms_norm_rows(x_ref[...], g_ref[...]).astype(BF16)
    qkv = _dot(h, w_ref[...])
    qk_width = Q_WIDTH + KV_WIDTH
    qk = qkv[:, :qk_width]
    v = qkv[:, qk_width:]
    sq_hi, sq_lo = _split_bf16(qk * qk)
    ss = _dot(sq_hi, psum_ref[...]) + _dot(sq_lo, psum_ref[...])
    inv_hi, inv_lo = _split_bf16(lax.rsqrt(ss * (1.0 / HEAD_DIM) + EPS))
    inv = _dot(inv_hi, pexp_ref[...]) + _dot(inv_lo, pexp_ref[...])
    qkn = qk * inv * gain_ref[...]
    cos, s1, s2 = cos_ref[...], s1_ref[...], s2_ref[...]
    half = ROT_DIM // 2
    for c in range(qk_width // LANES):
        blk = qkn[:, c * LANES:(c + 1) * LANES]
        rot = (blk * cos + pltpu.roll(blk, half, axis=1) * s1
               + pltpu.roll(blk, LANES - half, axis=1) * s2)
        if c < Q_WIDTH // LANES:
            q_ref[:, c * LANES:(c + 1) * LANES] = (rot * SCALE).astype(BF16)
        else:
            j = c - Q_WIDTH // LANES
            k_ref[:, j * LANES:(j + 1) * LANES] = rot
            khm_ref[2 * j] = rot[:, :HEAD_DIM].astype(BF16)
            khm_ref[2 * j + 1] = rot[:, HEAD_DIM:].astype(BF16)
    v_ref[...] = v
    ones_cols = jnp.ones((v.shape[0], HEAD_DIM), BF16)
    for h in range(N_KV):
        vhm_ref[h, :, :HEAD_DIM] = v[:, h * HEAD_DIM:(h + 1) * HEAD_DIM].astype(BF16)
        vhm_ref[h, :, HEAD_DIM:] = ones_cols


def _qkv_out(n_rows, tm, khm_width, vhm_width):
    shapes = (jax.ShapeDtypeStruct((n_rows, Q_WIDTH), BF16),
              jax.ShapeDtypeStruct((n_rows, KV_WIDTH), F32),
              jax.ShapeDtypeStruct((n_rows, KV_WIDTH), F32),
              jax.ShapeDtypeStruct((N_KV, n_rows, khm_width), BF16),
              jax.ShapeDtypeStruct((N_KV, n_rows, vhm_width), BF16))
    specs = (pl.BlockSpec((tm, Q_WIDTH), lambda i: (i, 0)),
             pl.BlockSpec((tm, KV_WIDTH), lambda i: (i, 0)),
             pl.BlockSpec((tm, KV_WIDTH), lambda i: (i, 0)),
             pl.BlockSpec((N_KV, tm, khm_width), lambda i: (0, i, 0)),
             pl.BlockSpec((N_KV, tm, vhm_width), lambda i: (0, i, 0)))
    return shapes, specs


def _row_tile(n_rows):
    return min(ROW_TILE, n_rows)


def _sb_qkv(x, gain, w):
    n_rows, d = x.shape
    tm = _row_tile(n_rows)
    shapes, specs = _qkv_out(n_rows, tm, 2 * HEAD_DIM, HEAD_DIM)
    return pl.pallas_call(
        _sb_qkv_kernel,
        grid=(n_rows // tm,),
        in_specs=[pl.BlockSpec((tm, d), lambda i: (i, 0)),
                  _const_spec((1, d)),
                  _const_spec(w.shape)],
        out_specs=specs, out_shape=shapes,
        compiler_params=_params(("arbitrary",)),
        name="sb_qkv",
    )(x, gain.reshape(1, d), w)


def _rope_tables(pos):
    half = ROT_DIM // 2
    inv_freq = ROPE_THETA ** (-jnp.arange(half, dtype=F32) * 2.0 / ROT_DIM)
    ang = pos.astype(F32)[:, None] * inv_freq[None, :]
    cos, sin = jnp.cos(ang), jnp.sin(ang)
    t = pos.shape[0]
    ones = jnp.ones((t, HEAD_DIM - ROT_DIM), F32)
    zeros_h = jnp.zeros((t, half), F32)
    zeros_r = jnp.zeros((t, HEAD_DIM - ROT_DIM), F32)
    c_head = jnp.concatenate([cos, cos, ones], axis=1)
    s1_head = jnp.concatenate([zeros_h, sin, zeros_r], axis=1)
    s2_head = jnp.concatenate([-sin, zeros_h, zeros_r], axis=1)
    two = lambda a: jnp.concatenate([a, a], axis=1)
    return two(c_head), two(s1_head), two(s2_head)


def _sw_qkv(x, gain, w, q_norm, k_norm, pos_tables, table_blocks):
    n_rows, d = x.shape
    tm = _row_tile(n_rows)
    shapes, specs = _qkv_out(n_rows, tm, HEAD_DIM, 2 * HEAD_DIM)
    qk_width = Q_WIDTH + KV_WIDTH
    head_of_lane = jnp.arange(qk_width) // HEAD_DIM
    psum = (head_of_lane[:, None] == jnp.arange(LANES)[None, :]).astype(BF16)
    pexp = psum.T
    gain_qk = jnp.concatenate([jnp.tile(q_norm, N_HEADS), jnp.tile(k_norm, N_KV)]).reshape(1, qk_width)
    tab_spec = pl.BlockSpec((tm, LANES), lambda i: (i % table_blocks, 0))
    return pl.pallas_call(
        _sw_qkv_kernel,
        grid=(n_rows // tm,),
        in_specs=[pl.BlockSpec((tm, d), lambda i: (i, 0)),
                  _const_spec((1, d)),
                  _const_spec(w.shape),
                  _const_spec((1, qk_width)),
                  _const_spec(psum.shape),
                  _const_spec(pexp.shape),
                  tab_spec, tab_spec, tab_spec],
        out_specs=specs, out_shape=shapes,
        compiler_params=_params(("arbitrary",)),
        name="sw_qkv",
    )(x, gain.reshape(1, d), w, gain_qk.astype(F32), psum, pexp, *pos_tables)


def _oproj_ffn_kernel(h_ref, o_ref, wo_ref, g_ref, wgu_ref, wd_ref, out_ref, *, d_ff, ff_chunk):
    h1 = h_ref[...] + _dot(o_ref[...], wo_ref[...])
    n = _rms_norm_rows(h1, g_ref[...]).astype(BF16)
    acc = h1
    for c in range(d_ff // ff_chunk):
        lo = c * ff_chunk
        gate = _dot(n, wgu_ref[:, lo:lo + ff_chunk])
        up = _dot(n, wgu_ref[:, d_ff + lo:d_ff + lo + ff_chunk])
        act = (gate * jax.nn.sigmoid(gate) * up).astype(BF16)
        acc = acc + _dot(act, wd_ref[lo:lo + ff_chunk, :])
    out_ref[...] = acc


def _ff_chunk(d_ff):
    for n_chunks in (2, 1):
        if d_ff % (n_chunks * LANES) == 0:
            return d_ff // n_chunks
    return d_ff


def _oproj_ffn(h, o, wo, gain, wgu, wd):
    n_rows, d = h.shape
    d_ff = wd.shape[0]
    tm = _row_tile(n_rows)
    kern = functools.partial(_oproj_ffn_kernel, d_ff=d_ff, ff_chunk=_ff_chunk(d_ff))
    return pl.pallas_call(
        kern,
        grid=(n_rows // tm,),
        in_specs=[pl.BlockSpec((tm, d), lambda i: (i, 0)),
                  pl.BlockSpec((tm, o.shape[1]), lambda i: (i, 0)),
                  _const_spec(wo.shape),
                  _const_spec((1, d)),
                  _const_spec(wgu.shape),
                  _const_spec(wd.shape)],
        out_specs=pl.BlockSpec((tm, d), lambda i: (i, 0)),
        out_shape=jax.ShapeDtypeStruct((n_rows, d), F32),
        compiler_params=_params(("arbitrary",)),
        name="oproj_ffn",
    )(h, o, wo, gain.reshape(1, d), wgu, wd)


def _sb_prompt_kernel(q_ref, k_ref, v_ref, bcol_ref, tri_ref, o_ref, qs_ref, acc_ref, r_ref,
                      z_buf, s_buf, a_buf):
    i = pl.program_id(2)
    tq = q_ref.shape[0]
    m = GROUP * tq
    q = q_ref[...]
    for g in range(GROUP):
        qs_ref[g * tq:(g + 1) * tq, :HEAD_DIM] = q[:, g * HEAD_DIM:(g + 1) * HEAD_DIM]
    qs_ref[:, HEAD_DIM:] = bcol_ref[...]
    tri = tri_ref[...]

    def visit(j, first, key_offset):
        start = pl.multiple_of(j * SB_TK, SB_TK)
        z = _dot_nt(qs_ref[...], k_ref[pl.ds(start, SB_TK), :])
        sp = _softplus(z)
        if key_offset is not None:
            t = lax.broadcasted_iota(jnp.int32, (m, SB_TK), 0) & (tq - 1)
            s = lax.broadcasted_iota(jnp.int32, (m, SB_TK), 1) + key_offset
            valid = s < t
            sp = jnp.where(valid, sp, 0.0)
        cum = _dot(sp.astype(BF16), tri)
        c = cum if first else cum + r_ref[...]
        a = jnp.exp(z - c)
        if key_offset is not None:
            a = jnp.where(valid, a, 0.0)
        av = _dot(a.astype(BF16), v_ref[pl.ds(start, SB_TK), :])
        if first:
            acc_ref[...] = av
            r_ref[...] = cum[:, 0:1]
        else:
            acc_ref[...] += av
            r_ref[...] += cum[:, 0:1]

    per_q = tq // SB_TK
    for kb in reversed(range(per_q)):
        visit(i * per_q + kb, kb == per_q - 1, kb * SB_TK)

    def key_rows(j):
        return pl.ds(pl.multiple_of(j * SB_TK, SB_TK), SB_TK)

    def stage_a(j, slot):
        z = _dot_nt(qs_ref[...], k_ref[key_rows(j), :])
        z_buf[slot] = z
        s_buf[slot] = _softplus(z).astype(BF16)

    def stage_b(slot):
        cum = _dot(s_buf[slot], tri)
        a_buf[slot] = jnp.exp(z_buf[slot] - (cum + r_ref[...])).astype(BF16)
        r_ref[...] += cum[:, 0:1]

    def stage_c(j, slot):
        acc_ref[...] += _dot(a_buf[slot], v_ref[key_rows(j), :])

    @pl.when(i > 0)
    def _():
        n_full = i * per_q
        blk = lambda tau: n_full - 1 - tau
        stage_a(blk(0), 0)
        stage_a(blk(1), 1)
        stage_b(0)

        def body(d, carry):
            tau = 2 + 2 * d
            stage_c(blk(tau - 2), 0)
            stage_b(1)
            stage_a(blk(tau), 0)
            stage_c(blk(tau - 1), 1)
            stage_b(0)
            stage_a(blk(tau + 1), 1)
            return carry

        lax.fori_loop(0, n_full // 2 - 1, body, 0)
        stage_c(blk(n_full - 2), 0)
        stage_b(1)
        stage_c(blk(n_full - 1), 1)

    acc = acc_ref[...]
    o_ref[...] = jnp.concatenate(
        [acc[g * tq:(g + 1) * tq, :] for g in range(GROUP)], axis=1).astype(BF16)


def _bias_columns(bias, rows_per_head):
    b = bias.astype(F32)
    hi = b.astype(BF16)
    mid = (b - hi.astype(F32)).astype(BF16)
    lo = (b - hi.astype(F32) - mid.astype(F32)).astype(BF16)
    cols = jnp.stack([hi, mid, lo], axis=-1)
    cols = jnp.pad(cols, ((0, 0), (0, HEAD_DIM - BIAS_TERMS)))
    cols = jnp.broadcast_to(cols.reshape(N_KV, GROUP, 1, HEAD_DIM), (N_KV, GROUP, rows_per_head, HEAD_DIM))
    return cols.reshape(N_KV, GROUP * rows_per_head, HEAD_DIM)


def _tri(n):
    j = jnp.arange(n)[:, None]
    s = jnp.arange(n)[None, :]
    return (j >= s).astype(BF16)


def _bias_rows(bias, rows_per_head):
    b = bias.astype(F32).reshape(N_KV, GROUP, 1, 1)
    return jnp.broadcast_to(b, (N_KV, GROUP, rows_per_head, BLK)).reshape(N_KV, GROUP * rows_per_head, BLK)


def _sb_prompt_attn(q, khm, vhm, bias):
    b, s, _ = q.shape
    gw = GROUP * HEAD_DIM
    tq = SB_TQ
    m = GROUP * tq
    return pl.pallas_call(
        _sb_prompt_kernel,
        grid=(b, N_KV, s // tq),
        in_specs=[pl.BlockSpec((None, tq, gw), lambda bi, h, i: (bi, i, h)),
                  pl.BlockSpec((None, None, s, 2 * HEAD_DIM), lambda bi, h, i: (h, bi, 0, 0)),
                  pl.BlockSpec((None, None, s, HEAD_DIM), lambda bi, h, i: (h, bi, 0, 0)),
                  pl.BlockSpec((None, m, HEAD_DIM), lambda bi, h, i: (h, 0, 0)),
                  _const_spec((SB_TK, SB_TK))],
        out_specs=pl.BlockSpec((None, tq, gw), lambda bi, h, i: (bi, i, h)),
        out_shape=jax.ShapeDtypeStruct((b, s, Q_WIDTH), BF16),
        scratch_shapes=[pltpu.VMEM((m, 2 * HEAD_DIM), BF16),
                        pltpu.VMEM((m, HEAD_DIM), F32),
                        pltpu.VMEM((m, 1), F32),
                        pltpu.VMEM((2, m, SB_TK), F32),
                        pltpu.VMEM((2, m, SB_TK), BF16),
                        pltpu.VMEM((2, m, SB_TK), BF16)],
        compiler_params=_params(("arbitrary", "arbitrary", "arbitrary")),
        name="sb_prompt_attn",
    )(q, khm, vhm, _bias_columns(bias, tq), _tri(SB_TK))


def _block_diag_q(q, n_seq, t):
    qh = q.reshape(n_seq, t, N_KV, GROUP, HEAD_DIM).transpose(0, 2, 3, 1, 4)
    qh = qh.reshape(n_seq, N_KV, GROUP * t, 1, HEAD_DIM)
    sel = (jnp.arange(N_KV)[:, None] == jnp.arange(N_KV)[None, :]).astype(q.dtype)
    out = qh * sel[None, :, None, :, None]
    return out.reshape(n_seq, N_HEADS * t, KV_WIDTH)


def _take_block_diag(acc, rows_per_kv):
    row_h = lax.broadcasted_iota(jnp.int32, (acc.shape[0], HEAD_DIM), 0) // rows_per_kv
    out = acc[:, :HEAD_DIM]
    for h in range(1, N_KV):
        out = jnp.where(row_h == h, acc[:, h * HEAD_DIM:(h + 1) * HEAD_DIM], out)
    return out


def _unstack_heads(o, n_seq, t):
    return o.reshape(n_seq, N_KV, GROUP, t, HEAD_DIM).transpose(0, 3, 1, 2, 4).reshape(n_seq * t, Q_WIDTH)


def _sb_sample_kernel(pt_ref, q_ref, bias_ref, knew_ref, vnew_ref, tri_ref, ck_ref, cv_ref, o_ref,
                      kbuf, vbuf, sem, *, n_seq, n_pages, pages_per_chunk, t_new):
    b = pl.program_id(0)
    n_chunks = n_pages // pages_per_chunk
    rows = q_ref.shape[0]

    total_chunks = n_seq * n_chunks
    lookahead = CHUNK_SLOTS - 1

    def page_copies(number, g):
        seq = lax.div(number, n_chunks)
        chunk = lax.rem(number, n_chunks)
        slot = lax.rem(number, CHUNK_SLOTS)
        page = pt_ref[seq * n_pages + (n_pages - 1 - (chunk * pages_per_chunk + g))]
        return (pltpu.make_async_copy(ck_ref.at[page], kbuf.at[slot, g], sem.at[0, slot]),
                pltpu.make_async_copy(cv_ref.at[page], vbuf.at[slot, g], sem.at[1, slot]))

    def start_chunk(number):
        for g in range(pages_per_chunk):
            k_copy, v_copy = page_copies(number, g)
            k_copy.start()
            v_copy.start(priority=1)

    def wait_chunk(number):
        for g in range(pages_per_chunk):
            for cp in page_copies(number, g):
                cp.wait()

    @pl.when(b == 0)
    def _():
        for number in range(min(lookahead, total_chunks)):
            start_chunk(jnp.int32(number))

    q = q_ref[...]
    bias = bias_ref[...]
    tri = tri_ref[...]

    def visit(kt_tiles, vt_tiles, r, valid):
        zs = [_dot(q, kt.astype(BF16)) + bias for kt in kt_tiles]
        sps = [_softplus(z) for z in zs]
        if valid is not None:
            sps = [jnp.where(valid, sp, 0.0) for sp in sps]
        cums = [_dot(sp.astype(BF16), tri) for sp in sps]
        av = None
        for z, cum, vt in zip(zs, cums, vt_tiles):
            a = jnp.exp(z - (cum + r))
            if valid is not None:
                a = jnp.where(valid, a, 0.0)
            term = _dot_nt(a.astype(BF16), vt.astype(BF16))
            av = term if av is None else av + term
            r = r + cum[:, 0:1]
        return av, r

    t_of_row = lax.broadcasted_iota(jnp.int32, (rows, BLK), 0) % t_new
    col = lax.broadcasted_iota(jnp.int32, (rows, BLK), 1)
    acc0, r0 = visit([knew_ref[...]], [vnew_ref[...]], jnp.zeros((rows, 1), F32), col < t_of_row)

    def chunk_body(chunk, carry):
        acc, r = carry
        number = b * n_chunks + chunk
        slot = lax.rem(number, CHUNK_SLOTS)

        @pl.when(number + lookahead < total_chunks)
        def _():
            start_chunk(number + lookahead)

        wait_chunk(number)
        av, r = visit([kbuf[slot, g] for g in range(pages_per_chunk)],
                      [vbuf[slot, g] for g in range(pages_per_chunk)], r, None)
        return acc + av, r

    acc, _ = lax.fori_loop(0, n_chunks, chunk_body, (acc0, r0))
    o_ref[...] = _take_block_diag(acc, rows // N_KV).astype(BF16)


def _pad_rows(x, rows):
    return jnp.pad(x, ((0, 0), (0, rows - x.shape[1]), (0, 0)))


def _transposed_page(x, n_seq, t):
    xt = x.reshape(n_seq, t, KV_WIDTH).transpose(0, 2, 1)
    return jnp.pad(xt, ((0, 0), (0, 0), (0, BLK - t)))


def _sb_sample_attn(q, k_new, v_new, bias, cache_kt, cache_vt, page_table):
    n_seq, n_pages = page_table.shape
    t = q.shape[0] // n_seq
    rows = N_HEADS * t
    qbd = _block_diag_q(q, n_seq, t)
    bias_rows = _bias_rows(bias, t).reshape(rows, BLK)
    ppc = PAGES_PER_CHUNK if n_pages % PAGES_PER_CHUNK == 0 else 1
    kern = functools.partial(_sb_sample_kernel, n_seq=n_seq, n_pages=n_pages, pages_per_chunk=ppc, t_new=t)
    grid_spec = pltpu.PrefetchScalarGridSpec(
        num_scalar_prefetch=1,
        grid=(n_seq,),
        in_specs=[pl.BlockSpec((None, rows, KV_WIDTH), lambda s, pt: (s, 0, 0)),
                  pl.BlockSpec((rows, BLK), lambda s, pt: (0, 0)),
                  pl.BlockSpec((None, KV_WIDTH, BLK), lambda s, pt: (s, 0, 0)),
                  pl.BlockSpec((None, KV_WIDTH, BLK), lambda s, pt: (s, 0, 0)),
                  pl.BlockSpec((BLK, BLK), lambda s, pt: (0, 0)),
                  pl.BlockSpec(memory_space=pl.ANY),
                  pl.BlockSpec(memory_space=pl.ANY)],
        out_specs=pl.BlockSpec((None, rows, HEAD_DIM), lambda s, pt: (s, 0, 0)),
        scratch_shapes=[pltpu.VMEM((CHUNK_SLOTS, ppc, KV_WIDTH, PAGE_SIZE), F32),
                        pltpu.VMEM((CHUNK_SLOTS, ppc, KV_WIDTH, PAGE_SIZE), F32),
                        pltpu.SemaphoreType.DMA((2, CHUNK_SLOTS))])
    o = pl.pallas_call(
        kern,
        grid_spec=grid_spec,
        out_shape=jax.ShapeDtypeStruct((n_seq, rows, HEAD_DIM), BF16),
        compiler_params=_params(("arbitrary",)),
        name="sb_sample_attn",
    )(page_table.reshape(-1), qbd, bias_rows, _transposed_page(k_new, n_seq, t),
      _transposed_page(v_new, n_seq, t), _tri(BLK), cache_kt, cache_vt)
    return _unstack_heads(o, n_seq, t)


def _sink_softmax_av(scores, values, sink):
    mx = sink
    for s in scores:
        mx = jnp.maximum(mx, jnp.max(s, axis=-1, keepdims=True))
    ps = [jnp.exp(s - mx) for s in scores]
    den = jnp.exp(sink - mx)
    for p in ps:
        den = den + jnp.sum(p, axis=-1, keepdims=True)
    inv = 1.0 / den
    out = None
    for p, v in zip(ps, values):
        term = _dot((p * inv).astype(BF16), v)
        out = term if out is None else out + term
    return out


def _sw_prompt_kernel(q_ref, kc_ref, kp_ref, vc_ref, vp_ref, sink_ref, sinkb_ref, o_ref, qs_ref, s_buf, mx_buf):
    i = pl.program_id(2)
    n_blocks = pl.num_programs(2) - 1
    m = GROUP * BLK
    cur = i % 2
    prv = 1 - cur

    @pl.when(i == 0)
    def _():
        s_buf[...] = jnp.zeros(s_buf.shape, F32)
        mx_buf[...] = jnp.zeros(mx_buf.shape, F32)

    mx = mx_buf[prv]
    p_prev = jnp.exp(s_buf[prv, :, :BLK] - mx).astype(BF16)
    p_cur = jnp.exp(s_buf[prv, :, BLK:] - mx).astype(BF16)
    av = _dot(p_prev, vp_ref[...]) + _dot(p_cur, vc_ref[...])
    den = pltpu.roll(av, HEAD_DIM, axis=1) + jnp.exp(sinkb_ref[...] - mx)
    out = (av * (1.0 / den))[:, :HEAD_DIM]
    o_ref[...] = jnp.concatenate(
        [out[g * BLK:(g + 1) * BLK, :] for g in range(GROUP)], axis=1).astype(BF16)
    sink = sink_ref[...]

    blk = jnp.minimum(i, n_blocks - 1)
    q = q_ref[...]
    for g in range(GROUP):
        qs_ref[g * BLK:(g + 1) * BLK, :] = q[:, g * HEAD_DIM:(g + 1) * HEAD_DIM]
    qs = qs_ref[...]
    t = lax.broadcasted_iota(jnp.int32, (m, BLK), 0) % BLK
    s = lax.broadcasted_iota(jnp.int32, (m, BLK), 1)
    neg = -jnp.inf
    s_cur = jnp.where(s <= t, _dot_nt(qs, kc_ref[...]), neg)
    first_key = t + jnp.where(blk > 0, 0, BLK)
    s_prev = jnp.where(s >= first_key, _dot_nt(qs, kp_ref[...]), neg)
    s_buf[cur, :, :BLK] = s_prev
    s_buf[cur, :, BLK:] = s_cur
    row_max = jnp.max(jnp.maximum(s_prev, s_cur), axis=-1, keepdims=True)
    mx_buf[cur] = jnp.broadcast_to(jnp.maximum(sink, row_max), (m, BLK))


def _sink_rows(sinks, rows_per_head):
    sk = sinks.astype(F32).reshape(N_KV, GROUP, 1, 1)
    return jnp.broadcast_to(sk, (N_KV, GROUP, rows_per_head, 1)).reshape(N_KV, GROUP * rows_per_head, 1)


def _sw_prompt_attn(q, khm, vhm, sinks):
    b, s, _ = q.shape
    gw = GROUP * HEAD_DIM
    m = GROUP * BLK
    nb = s // BLK
    score_blk = lambda i: jnp.minimum(i, nb - 1)
    kv_spec = lambda width, blk_of: pl.BlockSpec((None, None, BLK, width),
                                                 lambda bi, h, i: (h, bi, jnp.maximum(blk_of(i), 0), 0))
    sink_rows = _sink_rows(sinks, BLK)
    return pl.pallas_call(
        _sw_prompt_kernel,
        grid=(b, N_KV, nb + 1),
        in_specs=[pl.BlockSpec((None, BLK, gw), lambda bi, h, i: (bi, score_blk(i), h)),
                  kv_spec(HEAD_DIM, score_blk), kv_spec(HEAD_DIM, lambda i: score_blk(i) - 1),
                  kv_spec(2 * HEAD_DIM, lambda i: i - 1), kv_spec(2 * HEAD_DIM, lambda i: i - 2),
                  pl.BlockSpec((None, m, 1), lambda bi, h, i: (h, 0, 0)),
                  pl.BlockSpec((None, m, BLK), lambda bi, h, i: (h, 0, 0))],
        out_specs=pl.BlockSpec((None, BLK, gw), lambda bi, h, i: (bi, jnp.maximum(i - 1, 0), h)),
        out_shape=jax.ShapeDtypeStruct((b, s, Q_WIDTH), BF16),
        scratch_shapes=[pltpu.VMEM((m, HEAD_DIM), BF16),
                        pltpu.VMEM((2, m, 2 * BLK), F32),
                        pltpu.VMEM((2, m, BLK), F32)],
        compiler_params=_params(("arbitrary", "arbitrary", "arbitrary")),
        name="sw_prompt_attn",
    )(q, khm, khm, vhm, vhm, sink_rows, jnp.broadcast_to(sink_rows, (N_KV, m, BLK)))


def _sw_sample_kernel(q_ref, kk_ref, vv_ref, sink_ref, o_ref, *, t_new, n_buf):
    rows = q_ref.shape[1]
    n_keys = kk_ref.shape[1]
    t = lax.broadcasted_iota(jnp.int32, (rows, n_keys), 0) % t_new
    j = lax.broadcasted_iota(jnp.int32, (rows, n_keys), 1)
    valid = jnp.logical_and(j >= n_buf + t - WINDOW, j <= n_buf + t)
    sink = sink_ref[...]
    for i in range(q_ref.shape[0]):
        sc = jnp.where(valid, _dot_nt(q_ref[i], kk_ref[i].astype(BF16)), -jnp.inf)
        out = _sink_softmax_av([sc], [vv_ref[i].astype(BF16)], sink)
        o_ref[i] = _take_block_diag(out, rows // N_KV).astype(BF16)


def _sw_sample_attn(q, kk, vv, sinks, n_buf):
    n_seq = kk.shape[0]
    t = q.shape[0] // n_seq
    rows = N_HEADS * t
    n_keys = -(-kk.shape[1] // LANES) * LANES
    kk = _pad_rows(kk, n_keys)
    vv = _pad_rows(vv, n_keys)
    qbd = _block_diag_q(q, n_seq, t)
    sink_rows = _sink_rows(sinks, t).reshape(rows, 1)
    ns = SEQS_PER_STEP if n_seq % SEQS_PER_STEP == 0 else 1
    kern = functools.partial(_sw_sample_kernel, t_new=t, n_buf=n_buf)
    o = pl.pallas_call(
        kern,
        grid=(n_seq // ns,),
        in_specs=[pl.BlockSpec((ns, rows, KV_WIDTH), lambda s: (s, 0, 0)),
                  pl.BlockSpec((ns, n_keys, KV_WIDTH), lambda s: (s, 0, 0)),
                  pl.BlockSpec((ns, n_keys, KV_WIDTH), lambda s: (s, 0, 0)),
                  pl.BlockSpec((rows, 1), lambda s: (0, 0))],
        out_specs=pl.BlockSpec((ns, rows, HEAD_DIM), lambda s: (s, 0, 0)),
        out_shape=jax.ShapeDtypeStruct((n_seq, rows, HEAD_DIM), BF16),
        compiler_params=_params(("arbitrary",)),
        name="sw_sample_attn",
    )(qbd, kk, vv, sink_rows)
    return _unstack_heads(o, n_seq, t)


def kernel(x_prompt, x_sample, cache_k_sb, cache_v_sb, page_table, state_k_win, state_v_win,
           sb_norm, sb_w_qkv, sb_bias, sb_w_o, sw_norm, sw_w_qkv, sw_q_norm, sw_k_norm, sw_sinks, sw_w_o,
           ffn_norm, ffn_w_gate_up, ffn_w_down):
    b, s, d = x_prompt.shape
    n_seq, t_new, _ = x_sample.shape
    depth = ffn_norm.shape[0]
    past_len = page_table.shape[1] * PAGE_SIZE
    n_buf = state_k_win.shape[2]
    assert s % ROW_TILE == 0 and s % SB_TQ == 0 and SB_TQ == 2 * SB_TK

    hp = x_prompt.reshape(b * s, d)
    hs = x_sample.reshape(n_seq * t_new, d)
    bf = lambda w: w.astype(BF16)

    prompt_tables = _rope_tables(jnp.arange(s))
    sample_tables = _rope_tables(past_len + jnp.arange(t_new))
    sample_tables = tuple(jnp.tile(tab, (n_seq, 1)) for tab in sample_tables)

    outs = {name: [] for name in ("sb_kp", "sb_vp", "sb_ks", "sb_vs", "sw_kp", "sw_vp", "sw_ks", "sw_vs")}
    for i in range(depth):
        j = i // 2
        if i % 2 == 0:
            w_qkv = bf(sb_w_qkv[j])
            q, k, v, khm, vhm = _sb_qkv(hp, sb_norm[j], w_qkv)
            op = _sb_prompt_attn(q.reshape(b, s, Q_WIDTH), khm.reshape(N_KV, b, s, 2 * HEAD_DIM),
                                 vhm.reshape(N_KV, b, s, HEAD_DIM), sb_bias[j]).reshape(b * s, Q_WIDTH)
            qs, ks, vs, _, _ = _sb_qkv(hs, sb_norm[j], w_qkv)
            n_phys = cache_k_sb.shape[1]
            to_pages = lambda c: c.transpose(0, 2, 3, 1).reshape(n_phys, KV_WIDTH, PAGE_SIZE)
            os_ = _sb_sample_attn(qs, ks, vs, sb_bias[j], to_pages(cache_k_sb[j]), to_pages(cache_v_sb[j]),
                                  page_table)
            w_o = bf(sb_w_o[j])
            outs["sb_kp"].append(k.reshape(b, s, N_KV, HEAD_DIM))
            outs["sb_vp"].append(v.reshape(b, s, N_KV, HEAD_DIM))
            outs["sb_ks"].append(ks.reshape(n_seq, t_new, N_KV, HEAD_DIM))
            outs["sb_vs"].append(vs.reshape(n_seq, t_new, N_KV, HEAD_DIM))
        else:
            w_qkv = bf(sw_w_qkv[j])
            q, k, v, khm, vhm = _sw_qkv(hp, sw_norm[j], w_qkv, sw_q_norm[j], sw_k_norm[j],
                                        prompt_tables, s // _row_tile(b * s))
            op = _sw_prompt_attn(q.reshape(b, s, Q_WIDTH), khm.reshape(N_KV, b, s, HEAD_DIM),
                                 vhm.reshape(N_KV, b, s, 2 * HEAD_DIM), sw_sinks[j]).reshape(b * s, Q_WIDTH)
            qs, ks, vs, _, _ = _sw_qkv(hs, sw_norm[j], w_qkv, sw_q_norm[j], sw_k_norm[j],
                                       sample_tables, 1)
            kk = jnp.concatenate([state_k_win[j].reshape(n_seq, n_buf, KV_WIDTH),
                                  ks.reshape(n_seq, t_new, KV_WIDTH)], axis=1)
            vv = jnp.concatenate([state_v_win[j].reshape(n_seq, n_buf, KV_WIDTH),
                                  vs.reshape(n_seq, t_new, KV_WIDTH)], axis=1)
            os_ = _sw_sample_attn(qs, kk, vv, sw_sinks[j], n_buf)
            w_o = bf(sw_w_o[j])
            keep = min(WINDOW, s)
            last = lambda x: x.reshape(b, s, KV_WIDTH)[:, s - keep:, :].reshape(b, keep, N_KV, HEAD_DIM)
            outs["sw_kp"].append(last(k))
            outs["sw_vp"].append(last(v))
            outs["sw_ks"].append(kk[:, -n_buf:].reshape(n_seq, n_buf, N_KV, HEAD_DIM))
            outs["sw_vs"].append(vv[:, -n_buf:].reshape(n_seq, n_buf, N_KV, HEAD_DIM))
        w_gu, w_d = bf(ffn_w_gate_up[i]), bf(ffn_w_down[i])
        hp = _oproj_ffn(hp, op, w_o, ffn_norm[i], w_gu, w_d)
        hs = _oproj_ffn(hs, os_, w_o, ffn_norm[i], w_gu, w_d)

    return (hp.reshape(b, s, d), hs.reshape(n_seq, t_new, d),
            jnp.stack(outs["sb_kp"]), jnp.stack(outs["sb_vp"]),
            jnp.stack(outs["sb_ks"]), jnp.stack(outs["sb_vs"]),
            jnp.stack(outs["sw_kp"]), jnp.stack(outs["sw_vp"]),
            jnp.stack(outs["sw_ks"]), jnp.stack(outs["sw_vs"]))
```

```python
import functools

import jax
import jax.numpy as jnp
from jax import lax
from jax.experimental import pallas as pl
from jax.experimental.pallas import tpu as pltpu

F32 = jnp.float32
BF16 = jnp.bfloat16

HEAD_DIM = 64
N_HEADS = 16
N_KV = 4
GROUP = N_HEADS // N_KV
KV_WIDTH = N_KV * HEAD_DIM
Q_WIDTH = N_HEADS * HEAD_DIM
WINDOW = 128
ROT_DIM = HEAD_DIM // 4
ROPE_THETA = 500000.0
EPS = 1e-6
PAGE_SIZE = 128
SCALE = HEAD_DIM ** -0.5
LOG2E = 1.4426950408889634

LANES = 128
BLK = 128
SB_TQ = 512
SB_TK = 256
SW_QBLOCKS = 4
BIAS_TERMS = 3
ROW_TILE = 512
VMEM_LIMIT = 56 * 1024 * 1024
CHUNK_SLOTS = 3
PAGES_PER_CHUNK = 16
SEQS_PER_STEP = 8


def _params(sem):
    return pltpu.CompilerParams(dimension_semantics=sem, vmem_limit_bytes=VMEM_LIMIT)


def _const_spec(shape):
    nd = len(shape)
    return pl.BlockSpec(shape, lambda *_: (0,) * nd, pipeline_mode=pl.Buffered(1))


def _rms_norm_rows(x, gain):
    ms = jnp.mean(x * x, axis=-1, keepdims=True)
    return x * lax.rsqrt(ms + EPS) * gain


def _softplus(z):
    return jnp.maximum(z, 0.0) + jnp.log(1.0 + jnp.exp2(jnp.abs(z) * (-LOG2E)))


def _dot_nt(a, b):
    return lax.dot_general(a, b, (((1,), (1,)), ((), ())), preferred_element_type=F32)


def _dot(a, b):
    return jnp.dot(a, b, preferred_element_type=F32)


def _store_head_major(dst_ref, x):
    for h in range(N_KV):
        dst_ref[h] = x[:, h * HEAD_DIM:(h + 1) * HEAD_DIM].astype(BF16)


def _sb_qkv_kernel(x_ref, g_ref, w_ref, q_ref, k_ref, v_ref, khm_ref, vhm_ref):
    h = _rms_norm_rows(x_ref[...], g_ref[...]).astype(BF16)
    qkv = _dot(h, w_ref[...])
    q_ref[...] = (qkv[:, :Q_WIDTH] * SCALE).astype(BF16)
    k = qkv[:, Q_WIDTH:Q_WIDTH + KV_WIDTH]
    v = qkv[:, Q_WIDTH + KV_WIDTH:]
    k_ref[...] = k
    v_ref[...] = v
    lane = lax.broadcasted_iota(jnp.int32, (k.shape[0], HEAD_DIM), 1)
    ones_cols = jnp.where(lane < BIAS_TERMS, 1.0, 0.0).astype(BF16)
    for h in range(N_KV):
        khm_ref[h, :, :HEAD_DIM] = k[:, h * HEAD_DIM:(h + 1) * HEAD_DIM].astype(BF16)
        khm_ref[h, :, HEAD_DIM:] = ones_cols
    _store_head_major(vhm_ref, v)


def _split_bf16(x):
    hi = x.astype(BF16)
    lo = (x - hi.astype(F32)).astype(BF16)
    return hi, lo


def _sw_qkv_kernel(x_ref, g_ref, w_ref, gain_ref, psum_ref, pexp_ref, cos_ref, s1_ref, s2_ref,
                   q_ref, k_ref, v_ref, khm_ref, vhm_ref):
    h = _rms_norm_rows(x_ref[...], g_ref[...]).astype(BF16)
    qkv = _dot(h, w_ref[...])
    qk_width = Q_WIDTH + KV_WIDTH
    qk = qkv[:, :qk_width]
    v = qkv[:, qk_width:]
    sq_hi, sq_lo = _split_bf16(qk * qk)
    ss = _dot(sq_hi, psum_ref[...]) + _dot(sq_lo, psum_ref[...])
    inv_hi, inv_lo = _split_bf16(lax.rsqrt(ss * (1.0 / HEAD_DIM) + EPS))
    inv = _dot(inv_hi, pexp_ref[...]) + _dot(inv_lo, pexp_ref[...])
    qkn = qk * inv * gain_ref[...]
    cos, s1, s2 = cos_ref[...], s1_ref[...], s2_ref[...]
    half = ROT_DIM // 2
    for c in range(qk_width // LANES):
        blk = qkn[:, c * LANES:(c + 1) * LANES]
        rot = (blk * cos + pltpu.roll(blk, half, axis=1) * s1
               + pltpu.roll(blk, LANES - half, axis=1) * s2)
        if c < Q_WIDTH // LANES:
            q_ref[:, c * LANES:(c + 1) * LANES] = (rot * SCALE).astype(BF16)
        else:
            j = c - Q_WIDTH // LANES
            k_ref[:, j * LANES:(j + 1) * LANES] = rot
            khm_ref[2 * j] = rot[:, :HEAD_DIM].astype(BF16)
            khm_ref[2 * j + 1] = rot[:, HEAD_DIM:].astype(BF16)
    v_ref[...] = v
    ones_cols = jnp.ones((v.shape[0], HEAD_DIM), BF16)
    for h in range(N_KV):
        vhm_ref[h, :, :HEAD_DIM] = v[:, h * HEAD_DIM:(h + 1) * HEAD_DIM].astype(BF16)
        vhm_ref[h, :, HEAD_DIM:] = ones_cols


def _qkv_out(n_rows, tm, khm_width, vhm_width):
    shapes = (jax.ShapeDtypeStruct((n_rows, Q_WIDTH), BF16),
              jax.ShapeDtypeStruct((n_rows, KV_WIDTH), F32),
              jax.ShapeDtypeStruct((n_rows, KV_WIDTH), F32),
              jax.ShapeDtypeStruct((N_KV, n_rows, khm_width), BF16),
              jax.ShapeDtypeStruct((N_KV, n_rows, vhm_width), BF16))
    specs = (pl.BlockSpec((tm, Q_WIDTH), lambda i: (i, 0)),
             pl.BlockSpec((tm, KV_WIDTH), lambda i: (i, 0)),
             pl.BlockSpec((tm, KV_WIDTH), lambda i: (i, 0)),
             pl.BlockSpec((N_KV, tm, khm_width), lambda i: (0, i, 0)),
             pl.BlockSpec((N_KV, tm, vhm_width), lambda i: (0, i, 0)))
    return shapes, specs


def _row_tile(n_rows):
    return min(ROW_TILE, n_rows)


def _sb_qkv(x, gain, w):
    n_rows, d = x.shape
    tm = _row_tile(n_rows)
    shapes, specs = _qkv_out(n_rows, tm, 2 * HEAD_DIM, HEAD_DIM)
    return pl.pallas_call(
        _sb_qkv_kernel,
        grid=(n_rows // tm,),
        in_specs=[pl.BlockSpec((tm, d), lambda i: (i, 0)),
                  _const_spec((1, d)),
                  _const_spec(w.shape)],
        out_specs=specs, out_shape=shapes,
        compiler_params=_params(("arbitrary",)),
        name="sb_qkv",
    )(x, gain.reshape(1, d), w)


def _rope_tables(pos):
    half = ROT_DIM // 2
    inv_freq = ROPE_THETA ** (-jnp.arange(half, dtype=F32) * 2.0 / ROT_DIM)
    ang = pos.astype(F32)[:, None] * inv_freq[None, :]
    cos, sin = jnp.cos(ang), jnp.sin(ang)
    t = pos.shape[0]
    ones = jnp.ones((t, HEAD_DIM - ROT_DIM), F32)
    zeros_h = jnp.zeros((t, half), F32)
    zeros_r = jnp.zeros((t, HEAD_DIM - ROT_DIM), F32)
    c_head = jnp.concatenate([cos, cos, ones], axis=1)
    s1_head = jnp.concatenate([zeros_h, sin, zeros_r], axis=1)
    s2_head = jnp.concatenate([-sin, zeros_h, zeros_r], axis=1)
    two = lambda a: jnp.concatenate([a, a], axis=1)
    return two(c_head), two(s1_head), two(s2_head)


def _sw_qkv(x, gain, w, q_norm, k_norm, pos_tables, table_blocks):
    n_rows, d = x.shape
    tm = _row_tile(n_rows)
    shapes, specs = _qkv_out(n_rows, tm, HEAD_DIM, 2 * HEAD_DIM)
    qk_width = Q_WIDTH + KV_WIDTH
    head_of_lane = jnp.arange(qk_width) // HEAD_DIM
    psum = (head_of_lane[:, None] == jnp.arange(LANES)[None, :]).astype(BF16)
    pexp = psum.T
    gain_qk = jnp.concatenate([jnp.tile(q_norm, N_HEADS), jnp.tile(k_norm, N_KV)]).reshape(1, qk_width)
    tab_spec = pl.BlockSpec((tm, LANES), lambda i: (i % table_blocks, 0))
    return pl.pallas_call(
        _sw_qkv_kernel,
        grid=(n_rows // tm,),
        in_specs=[pl.BlockSpec((tm, d), lambda i: (i, 0)),
                  _const_spec((1, d)),
                  _const_spec(w.shape),
                  _const_spec((1, qk_width)),
                  _const_spec(psum.shape),
                  _const_spec(pexp.shape),
                  tab_spec, tab_spec, tab_spec],
        out_specs=specs, out_shape=shapes,
        compiler_params=_params(("arbitrary",)),
        name="sw_qkv",
    )(x, gain.reshape(1, d), w, gain_qk.astype(F32), psum, pexp, *pos_tables)


def _oproj_ffn_kernel(h_ref, o_ref, wo_ref, g_ref, wgu_ref, wd_ref, out_ref, *, d_ff, ff_chunk):
    h1 = h_ref[...] + _dot(o_ref[...], wo_ref[...])
    n = _rms_norm_rows(h1, g_ref[...]).astype(BF16)
    acc = h1
    for c in range(d_ff // ff_chunk):
        lo = c * ff_chunk
        gate = _dot(n, wgu_ref[:, lo:lo + ff_chunk])
        up = _dot(n, wgu_ref[:, d_ff + lo:d_ff + lo + ff_chunk])
        act = (gate * jax.nn.sigmoid(gate) * up).astype(BF16)
        acc = acc + _dot(act, wd_ref[lo:lo + ff_chunk, :])
    out_ref[...] = acc


def _ff_chunk(d_ff):
    for n_chunks in (2, 1):
        if d_ff % (n_chunks * LANES) == 0:
            return d_ff // n_chunks
    return d_ff


def _oproj_ffn(h, o, wo, gain, wgu, wd):
    n_rows, d = h.shape
    d_ff = wd.shape[0]
    tm = _row_tile(n_rows)
    kern = functools.partial(_oproj_ffn_kernel, d_ff=d_ff, ff_chunk=_ff_chunk(d_ff))
    return pl.pallas_call(
        kern,
        grid=(n_rows // tm,),
        in_specs=[pl.BlockSpec((tm, d), lambda i: (i, 0)),
                  pl.BlockSpec((tm, o.shape[1]), lambda i: (i, 0)),
                  _const_spec(wo.shape),
                  _const_spec((1, d)),
                  _const_spec(wgu.shape),
                  _const_spec(wd.shape)],
        out_specs=pl.BlockSpec((tm, d), lambda i: (i, 0)),
        out_shape=jax.ShapeDtypeStruct((n_rows, d), F32),
        compiler_params=_params(("arbitrary",)),
        name="oproj_ffn",
    )(h, o, wo, gain.reshape(1, d), wgu, wd)


def _sb_prompt_kernel(q_ref, k_ref, v_ref, bcol_ref, tri_ref, o_ref, qs_ref, acc_ref, r_ref,
                      z_buf, s_buf, a_buf):
    i = pl.program_id(2)
    tq = q_ref.shape[0]
    m = GROUP * tq
    q = q_ref[...]
    for g in range(GROUP):
        qs_ref[g * tq:(g + 1) * tq, :HEAD_DIM] = q[:, g * HEAD_DIM:(g + 1) * HEAD_DIM]
    qs_ref[:, HEAD_DIM:] = bcol_ref[...]
    tri = tri_ref[...]

    def visit(j, first, key_offset):
        start = pl.multiple_of(j * SB_TK, SB_TK)
        z = _dot_nt(qs_ref[...], k_ref[pl.ds(start, SB_TK), :])
        sp = _softplus(z)
        if key_offset is not None:
            t = lax.broadcasted_iota(jnp.int32, (m, SB_TK), 0) & (tq - 1)
            s = lax.broadcasted_iota(jnp.int32, (m, SB_TK), 1) + key_offset
            valid = s < t
            sp = jnp.where(valid, sp, 0.0)
        cum = _dot(sp.astype(BF16), tri)
        c = cum if first else cum + r_ref[...]
        a = jnp.exp(z - c)
        if key_offset is not None:
            a = jnp.where(valid, a, 0.0)
        av = _dot(a.astype(BF16), v_ref[pl.ds(start, SB_TK), :])
        if first:
            acc_ref[...] = av
            r_ref[...] = cum[:, 0:1]
        else:
            acc_ref[...] += av
            r_ref[...] += cum[:, 0:1]

    per_q = tq // SB_TK
    for kb in reversed(range(per_q)):
        visit(i * per_q + kb, kb == per_q - 1, kb * SB_TK)

    def key_rows(j):
        return pl.ds(pl.multiple_of(j * SB_TK, SB_TK), SB_TK)

    def stage_a(j, slot):
        z = _dot_nt(qs_ref[...], k_ref[key_rows(j), :])
        z_buf[slot] = z
        s_buf[slot] = _softplus(z).astype(BF16)

    def stage_b(slot):
        cum = _dot(s_buf[slot], tri)
        a_buf[slot] = jnp.exp(z_buf[slot] - (cum + r_ref[...])).astype(BF16)
        r_ref[...] += cum[:, 0:1]

    def stage_c(j, slot):
        acc_ref[...] += _dot(a_buf[slot], v_ref[key_rows(j), :])

    @pl.when(i > 0)
    def _():
        n_full = i * per_q
        blk = lambda tau: n_full - 1 - tau
        stage_a(blk(0), 0)
        stage_a(blk(1), 1)
        stage_b(0)

        def body(d, carry):
            tau = 2 + 2 * d
            stage_c(blk(tau - 2), 0)
            stage_b(1)
            stage_a(blk(tau), 0)
            stage_c(blk(tau - 1), 1)
            stage_b(0)
            stage_a(blk(tau + 1), 1)
            return carry

        lax.fori_loop(0, n_full // 2 - 1, body, 0)
        stage_c(blk(n_full - 2), 0)
        stage_b(1)
        stage_c(blk(n_full - 1), 1)

    acc = acc_ref[...]
    o_ref[...] = jnp.concatenate(
        [acc[g * tq:(g + 1) * tq, :] for g in range(GROUP)], axis=1).astype(BF16)


def _bias_columns(bias, rows_per_head):
    b = bias.astype(F32)
    hi = b.astype(BF16)
    mid = (b - hi.astype(F32)).astype(BF16)
    lo = (b - hi.astype(F32) - mid.astype(F32)).astype(BF16)
    cols = jnp.stack([hi, mid, lo], axis=-1)
    cols = jnp.pad(cols, ((0, 0), (0, HEAD_DIM - BIAS_TERMS)))
    cols = jnp.broadcast_to(cols.reshape(N_KV, GROUP, 1, HEAD_DIM), (N_KV, GROUP, rows_per_head, HEAD_DIM))
    return cols.reshape(N_KV, GROUP * rows_per_head, HEAD_DIM)


def _tri(n):
    j = jnp.arange(n)[:, None]
    s = jnp.arange(n)[None, :]
    return (j >= s).astype(BF16)


def _bias_rows(bias, rows_per_head):
    b = bias.astype(F32).reshape(N_KV, GROUP, 1, 1)
    return jnp.broadcast_to(b, (N_KV, GROUP, rows_per_head, BLK)).reshape(N_KV, GROUP * rows_per_head, BLK)


def _sb_prompt_attn(q, khm, vhm, bias):
    b, s, _ = q.shape
    gw = GROUP * HEAD_DIM
    tq = SB_TQ
    m = GROUP * tq
    return pl.pallas_call(
        _sb_prompt_kernel,
        grid=(b, N_KV, s // tq),
        in_specs=[pl.BlockSpec((None, tq, gw), lambda bi, h, i: (bi, i, h)),
                  pl.BlockSpec((None, None, s, 2 * HEAD_DIM), lambda bi, h, i: (h, bi, 0, 0)),
                  pl.BlockSpec((None, None, s, HEAD_DIM), lambda bi, h, i: (h, bi, 0, 0)),
                  pl.BlockSpec((None, m, HEAD_DIM), lambda bi, h, i: (h, 0, 0)),
                  _const_spec((SB_TK, SB_TK))],
        out_specs=pl.BlockSpec((None, tq, gw), lambda bi, h, i: (bi, i, h)),
        out_shape=jax.ShapeDtypeStruct((b, s, Q_WIDTH), BF16),
        scratch_shapes=[pltpu.VMEM((m, 2 * HEAD_DIM), BF16),
                        pltpu.VMEM((m, HEAD_DIM), F32),
                        pltpu.VMEM((m, 1), F32),
                        pltpu.VMEM((2, m, SB_TK), F32),
                        pltpu.VMEM((2, m, SB_TK), BF16),
                        pltpu.VMEM((2, m, SB_TK), BF16)],
        compiler_params=_params(("arbitrary", "arbitrary", "arbitrary")),
        name="sb_prompt_attn",
    )(q, khm, vhm, _bias_columns(bias, tq), _tri(SB_TK))


def _block_diag_q(q, n_seq, t):
    qh = q.reshape(n_seq, t, N_KV, GROUP, HEAD_DIM).transpose(0, 2, 3, 1, 4)
    qh = qh.reshape(n_seq, N_KV, GROUP * t, 1, HEAD_DIM)
    sel = (jnp.arange(N_KV)[:, None] == jnp.arange(N_KV)[None, :]).astype(q.dtype)
    out = qh * sel[None, :, None, :, None]
    return out.reshape(n_seq, N_HEADS * t, KV_WIDTH)


def _take_block_diag(acc, rows_per_kv):
    row_h = lax.broadcasted_iota(jnp.int32, (acc.shape[0], HEAD_DIM), 0) // rows_per_kv
    out = acc[:, :HEAD_DIM]
    for h in range(1, N_KV):
        out = jnp.where(row_h == h, acc[:, h * HEAD_DIM:(h + 1) * HEAD_DIM], out)
    return out


def _unstack_heads(o, n_seq, t):
    return o.reshape(n_seq, N_KV, GROUP, t, HEAD_DIM).transpose(0, 3, 1, 2, 4).reshape(n_seq * t, Q_WIDTH)


def _sb_sample_kernel(pt_ref, q_ref, bias_ref, knew_ref, vnew_ref, tri_ref, ck_ref, cv_ref, o_ref,
                      kbuf, vbuf, sem, *, n_seq, n_pages, pages_per_chunk, t_new):
    b = pl.program_id(0)
    n_chunks = n_pages // pages_per_chunk
    rows = q_ref.shape[0]

    total_chunks = n_seq * n_chunks
    lookahead = CHUNK_SLOTS - 1

    def page_copies(number, g):
        seq = lax.div(number, n_chunks)
        chunk = lax.rem(number, n_chunks)
        slot = lax.rem(number, CHUNK_SLOTS)
        page = pt_ref[seq * n_pages + (n_pages - 1 - (chunk * pages_per_chunk + g))]
        return (pltpu.make_async_copy(ck_ref.at[page], kbuf.at[slot, g], sem.at[0, slot]),
                pltpu.make_async_copy(cv_ref.at[page], vbuf.at[slot, g], sem.at[1, slot]))

    def start_chunk(number):
        for g in range(pages_per_chunk):
            k_copy, v_copy = page_copies(number, g)
            k_copy.start()
            v_copy.start(priority=1)

    def wait_chunk(number):
        for g in range(pages_per_chunk):
            for cp in page_copies(number, g):
                cp.wait()

    @pl.when(b == 0)
    def _():
        for number in range(min(lookahead, total_chunks)):
            start_chunk(jnp.int32(number))

    q = q_ref[...]
    bias = bias_ref[...]
    tri = tri_ref[...]

    def visit(kt_tiles, vt_tiles, r, valid):
        zs = [_dot(q, kt.astype(BF16)) + bias for kt in kt_tiles]
        sps = [_softplus(z) for z in zs]
        if valid is not None:
            sps = [jnp.where(valid, sp, 0.0) for sp in sps]
        cums = [_dot(sp.astype(BF16), tri) for sp in sps]
        av = None
        for z, cum, vt in zip(zs, cums, vt_tiles):
            a = jnp.exp(z - (cum + r))
            if valid is not None:
                a = jnp.where(valid, a, 0.0)
            term = _dot_nt(a.astype(BF16), vt.astype(BF16))
            av = term if av is None else av + term
            r = r + cum[:, 0:1]
        return av, r

    t_of_row = lax.broadcasted_iota(jnp.int32, (rows, BLK), 0) % t_new
    col = lax.broadcasted_iota(jnp.int32, (rows, BLK), 1)
    acc0, r0 = visit([knew_ref[...]], [vnew_ref[...]], jnp.zeros((rows, 1), F32), col < t_of_row)

    def chunk_body(chunk, carry):
        acc, r = carry
        number = b * n_chunks + chunk
        slot = lax.rem(number, CHUNK_SLOTS)

        @pl.when(number + lookahead < total_chunks)
        def _():
            start_chunk(number + lookahead)

        wait_chunk(number)
        av, r = visit([kbuf[slot, g] for g in range(pages_per_chunk)],
                      [vbuf[slot, g] for g in range(pages_per_chunk)], r, None)
        return acc + av, r

    acc, _ = lax.fori_loop(0, n_chunks, chunk_body, (acc0, r0))
    o_ref[...] = _take_block_diag(acc, rows // N_KV).astype(BF16)


def _pad_rows(x, rows):
    return jnp.pad(x, ((0, 0), (0, rows - x.shape[1]), (0, 0)))


def _transposed_page(x, n_seq, t):
    xt = x.reshape(n_seq, t, KV_WIDTH).transpose(0, 2, 1)
    return jnp.pad(xt, ((0, 0), (0, 0), (0, BLK - t)))


def _sb_sample_attn(q, k_new, v_new, bias, cache_kt, cache_vt, page_table):
    n_seq, n_pages = page_table.shape
    t = q.shape[0] // n_seq
    rows = N_HEADS * t
    qbd = _block_diag_q(q, n_seq, t)
    bias_rows = _bias_rows(bias, t).reshape(rows, BLK)
    ppc = PAGES_PER_CHUNK if n_pages % PAGES_PER_CHUNK == 0 else 1
    kern = functools.partial(_sb_sample_kernel, n_seq=n_seq, n_pages=n_pages, pages_per_chunk=ppc, t_new=t)
    grid_spec = pltpu.PrefetchScalarGridSpec(
        num_scalar_prefetch=1,
        grid=(n_seq,),
        in_specs=[pl.BlockSpec((None, rows, KV_WIDTH), lambda s, pt: (s, 0, 0)),
                  pl.BlockSpec((rows, BLK), lambda s, pt: (0, 0)),
                  pl.BlockSpec((None, KV_WIDTH, BLK), lambda s, pt: (s, 0, 0)),
                  pl.BlockSpec((None, KV_WIDTH, BLK), lambda s, pt: (s, 0, 0)),
                  pl.BlockSpec((BLK, BLK), lambda s, pt: (0, 0)),
                  pl.BlockSpec(memory_space=pl.ANY),
                  pl.BlockSpec(memory_space=pl.ANY)],
        out_specs=pl.BlockSpec((None, rows, HEAD_DIM), lambda s, pt: (s, 0, 0)),
        scratch_shapes=[pltpu.VMEM((CHUNK_SLOTS, ppc, KV_WIDTH, PAGE_SIZE), F32),
                        pltpu.VMEM((CHUNK_SLOTS, ppc, KV_WIDTH, PAGE_SIZE), F32),
                        pltpu.SemaphoreType.DMA((2, CHUNK_SLOTS))])
    o = pl.pallas_call(
        kern,
        grid_spec=grid_spec,
        out_shape=jax.ShapeDtypeStruct((n_seq, rows, HEAD_DIM), BF16),
        compiler_params=_params(("arbitrary",)),
        name="sb_sample_attn",
    )(page_table.reshape(-1), qbd, bias_rows, _transposed_page(k_new, n_seq, t),
      _transposed_page(v_new, n_seq, t), _tri(BLK), cache_kt, cache_vt)
    return _unstack_heads(o, n_seq, t)


def _sink_softmax_av(scores, values, sink):
    mx = sink
    for s in scores:
        mx = jnp.maximum(mx, jnp.max(s, axis=-1, keepdims=True))
    ps = [jnp.exp(s - mx) for s in scores]
    den = jnp.exp(sink - mx)
    for p in ps:
        den = den + jnp.sum(p, axis=-1, keepdims=True)
    inv = 1.0 / den
    out = None
    for p, v in zip(ps, values):
        term = _dot((p * inv).astype(BF16), v)
        out = term if out is None else out + term
    return out


def _sw_prompt_kernel(q_ref, kc_ref, kp_ref, vc_ref, vp_ref, sink_ref, sinkb_ref, o_ref, qs_ref):
    i = pl.program_id(2)
    m = GROUP * BLK
    q = q_ref[...]
    for sub in range(SW_QBLOCKS):
        for g in range(GROUP):
            r0 = (sub * GROUP + g) * BLK
            qs_ref[r0:r0 + BLK, :] = q[sub * BLK:(sub + 1) * BLK, g * HEAD_DIM:(g + 1) * HEAD_DIM]
    t = lax.broadcasted_iota(jnp.int32, (m, BLK), 0) % BLK
    s = lax.broadcasted_iota(jnp.int32, (m, BLK), 1)
    sink = sink_ref[...]
    sinkb = sinkb_ref[...]
    neg = -jnp.inf

    def own(ref, sub):
        return ref[sub * BLK:(sub + 1) * BLK, :]

    def before(ref, prev_ref, sub):
        return prev_ref[...] if sub == 0 else own(ref, sub - 1)

    def scores(sub):
        qs = qs_ref[sub * m:(sub + 1) * m, :]
        s_cur = jnp.where(s <= t, _dot_nt(qs, own(kc_ref, sub)), neg)
        first_key = t + jnp.where(i > 0, 0, BLK) if sub == 0 else t
        s_prev = jnp.where(s >= first_key, _dot_nt(qs, before(kc_ref, kp_ref, sub)), neg)
        row_max = jnp.max(jnp.maximum(s_prev, s_cur), axis=-1, keepdims=True)
        return s_prev, s_cur, jnp.broadcast_to(jnp.maximum(sink, row_max), (m, BLK))

    def finish(sub, s_prev, s_cur, mx):
        av = (_dot(jnp.exp(s_prev - mx).astype(BF16), before(vc_ref, vp_ref, sub))
              + _dot(jnp.exp(s_cur - mx).astype(BF16), own(vc_ref, sub)))
        den = pltpu.roll(av, HEAD_DIM, axis=1) + jnp.exp(sinkb - mx)
        out = (av * (1.0 / den))[:, :HEAD_DIM]
        o_ref[sub * BLK:(sub + 1) * BLK, :] = jnp.concatenate(
            [out[g * BLK:(g + 1) * BLK, :] for g in range(GROUP)], axis=1).astype(BF16)

    pending = None
    for sub in range(SW_QBLOCKS):
        scored = scores(sub)
        if pending is not None:
            finish(*pending)
        pending = (sub,) + scored
    finish(*pending)


def _sink_rows(sinks, rows_per_head):
    sk = sinks.astype(F32).reshape(N_KV, GROUP, 1, 1)
    return jnp.broadcast_to(sk, (N_KV, GROUP, rows_per_head, 1)).reshape(N_KV, GROUP * rows_per_head, 1)


def _sw_prompt_attn(q, khm, vhm, sinks):
    b, s, _ = q.shape
    gw = GROUP * HEAD_DIM
    m = GROUP * BLK
    tq = SW_QBLOCKS * BLK
    own = lambda width: pl.BlockSpec((None, None, tq, width), lambda bi, h, i: (h, bi, i, 0))
    before = lambda width: pl.BlockSpec((None, None, BLK, width),
                                        lambda bi, h, i: (h, bi, jnp.maximum(i * SW_QBLOCKS - 1, 0), 0))
    sink_rows = _sink_rows(sinks, BLK)
    return pl.pallas_call(
        _sw_prompt_kernel,
        grid=(b, N_KV, s // tq),
        in_specs=[pl.BlockSpec((None, tq, gw), lambda bi, h, i: (bi, i, h)),
                  own(HEAD_DIM), before(HEAD_DIM), own(2 * HEAD_DIM), before(2 * HEAD_DIM),
                  pl.BlockSpec((None, m, 1), lambda bi, h, i: (h, 0, 0)),
                  pl.BlockSpec((None, m, BLK), lambda bi, h, i: (h, 0, 0))],
        out_specs=pl.BlockSpec((None, tq, gw), lambda bi, h, i: (bi, i, h)),
        out_shape=jax.ShapeDtypeStruct((b, s, Q_WIDTH), BF16),
        scratch_shapes=[pltpu.VMEM((SW_QBLOCKS * m, HEAD_DIM), BF16)],
        compiler_params=_params(("arbitrary", "arbitrary", "arbitrary")),
        name="sw_prompt_attn",
    )(q, khm, khm, vhm, vhm, sink_rows, jnp.broadcast_to(sink_rows, (N_KV, m, BLK)))


def _sw_sample_kernel(q_ref, kk_ref, vv_ref, sink_ref, o_ref, *, t_new, n_buf):
    rows = q_ref.shape[1]
    n_keys = kk_ref.shape[1]
    t = lax.broadcasted_iota(jnp.int32, (rows, n_keys), 0) % t_new
    j = lax.broadcasted_iota(jnp.int32, (rows, n_keys), 1)
    valid = jnp.logical_and(j >= n_buf + t - WINDOW, j <= n_buf + t)
    sink = sink_ref[...]
    for i in range(q_ref.shape[0]):
        sc = jnp.where(valid, _dot_nt(q_ref[i], kk_ref[i].astype(BF16)), -jnp.inf)
        out = _sink_softmax_av([sc], [vv_ref[i].astype(BF16)], sink)
        o_ref[i] = _take_block_diag(out, rows // N_KV).astype(BF16)


def _sw_sample_attn(q, kk, vv, sinks, n_buf):
    n_seq = kk.shape[0]
    t = q.shape[0] // n_seq
    rows = N_HEADS * t
    n_keys = -(-kk.shape[1] // LANES) * LANES
    kk = _pad_rows(kk, n_keys)
    vv = _pad_rows(vv, n_keys)
    qbd = _block_diag_q(q, n_seq, t)
    sink_rows = _sink_rows(sinks, t).reshape(rows, 1)
    ns = SEQS_PER_STEP if n_seq % SEQS_PER_STEP == 0 else 1
    kern = functools.partial(_sw_sample_kernel, t_new=t, n_buf=n_buf)
    o = pl.pallas_call(
        kern,
        grid=(n_seq // ns,),
        in_specs=[pl.BlockSpec((ns, rows, KV_WIDTH), lambda s: (s, 0, 0)),
                  pl.BlockSpec((ns, n_keys, KV_WIDTH), lambda s: (s, 0, 0)),
                  pl.BlockSpec((ns, n_keys, KV_WIDTH), lambda s: (s, 0, 0)),
                  pl.BlockSpec((rows, 1), lambda s: (0, 0))],
        out_specs=pl.BlockSpec((ns, rows, HEAD_DIM), lambda s: (s, 0, 0)),
        out_shape=jax.ShapeDtypeStruct((n_seq, rows, HEAD_DIM), BF16),
        compiler_params=_params(("arbitrary",)),
        name="sw_sample_attn",
    )(qbd, kk, vv, sink_rows)
    return _unstack_heads(o, n_seq, t)


def kernel(x_prompt, x_sample, cache_k_sb, cache_v_sb, page_table, state_k_win, state_v_win,
           sb_norm, sb_w_qkv, sb_bias, sb_w_o, sw_norm, sw_w_qkv, sw_q_norm, sw_k_norm, sw_sinks, sw_w_o,
           ffn_norm, ffn_w_gate_up, ffn_w_down):
    b, s, d = x_prompt.shape
    n_seq, t_new, _ = x_sample.shape
    depth = ffn_norm.shape[0]
    past_len = page_table.shape[1] * PAGE_SIZE
    n_buf = state_k_win.shape[2]
    assert s % ROW_TILE == 0 and s % SB_TQ == 0 and SB_TQ == 2 * SB_TK

    hp = x_prompt.reshape(b * s, d)
    hs = x_sample.reshape(n_seq * t_new, d)
    bf = lambda w: w.astype(BF16)

    prompt_tables = _rope_tables(jnp.arange(s))
    sample_tables = _rope_tables(past_len + jnp.arange(t_new))
    sample_tables = tuple(jnp.tile(tab, (n_seq, 1)) for tab in sample_tables)

    outs = {name: [] for name in ("sb_kp", "sb_vp", "sb_ks", "sb_vs", "sw_kp", "sw_vp", "sw_ks", "sw_vs")}
    for i in range(depth):
        j = i // 2
        if i % 2 == 0:
            w_qkv = bf(sb_w_qkv[j])
            q, k, v, khm, vhm = _sb_qkv(hp, sb_norm[j], w_qkv)
            op = _sb_prompt_attn(q.reshape(b, s, Q_WIDTH), khm.reshape(N_KV, b, s, 2 * HEAD_DIM),
                                 vhm.reshape(N_KV, b, s, HEAD_DIM), sb_bias[j]).reshape(b * s, Q_WIDTH)
            qs, ks, vs, _, _ = _sb_qkv(hs, sb_norm[j], w_qkv)
            n_phys = cache_k_sb.shape[1]
            to_pages = lambda c: c.transpose(0, 2, 3, 1).reshape(n_phys, KV_WIDTH, PAGE_SIZE)
            os_ = _sb_sample_attn(qs, ks, vs, sb_bias[j], to_pages(cache_k_sb[j]), to_pages(cache_v_sb[j]),
                                  page_table)
            w_o = bf(sb_w_o[j])
            outs["sb_kp"].append(k.reshape(b, s, N_KV, HEAD_DIM))
            outs["sb_vp"].append(v.reshape(b, s, N_KV, HEAD_DIM))
            outs["sb_ks"].append(ks.reshape(n_seq, t_new, N_KV, HEAD_DIM))
            outs["sb_vs"].append(vs.reshape(n_seq, t_new, N_KV, HEAD_DIM))
        else:
            w_qkv = bf(sw_w_qkv[j])
            q, k, v, khm, vhm = _sw_qkv(hp, sw_norm[j], w_qkv, sw_q_norm[j], sw_k_norm[j],
                                        prompt_tables, s // _row_tile(b * s))
            op = _sw_prompt_attn(q.reshape(b, s, Q_WIDTH), khm.reshape(N_KV, b, s, HEAD_DIM),
                                 vhm.reshape(N_KV, b, s, 2 * HEAD_DIM), sw_sinks[j]).reshape(b * s, Q_WIDTH)
            qs, ks, vs, _, _ = _sw_qkv(hs, sw_norm[j], w_qkv, sw_q_norm[j], sw_k_norm[j],
                                       sample_tables, 1)
            kk = jnp.concatenate([state_k_win[j].reshape(n_seq, n_buf, KV_WIDTH),
                                  ks.reshape(n_seq, t_new, KV_WIDTH)], axis=1)
            vv = jnp.concatenate([state_v_win[j].reshape(n_seq, n_buf, KV_WIDTH),
                                  vs.reshape(n_seq, t_new, KV_WIDTH)], axis=1)
            os_ = _sw_sample_attn(qs, kk, vv, sw_sinks[j], n_buf)
            w_o = bf(sw_w_o[j])
            keep = min(WINDOW, s)
            last = lambda x: x.reshape(b, s, KV_WIDTH)[:, s - keep:, :].reshape(b, keep, N_KV, HEAD_DIM)
            outs["sw_kp"].append(last(k))
            outs["sw_vp"].append(last(v))
            outs["sw_ks"].append(kk[:, -n_buf:].reshape(n_seq, n_buf, N_KV, HEAD_DIM))
            outs["sw_vs"].append(vv[:, -n_buf:].reshape(n_seq, n_buf, N_KV, HEAD_DIM))
        w_gu, w_d = bf(ffn_w_gate_up[i]), bf(ffn_w_down[i])
        hp = _oproj_ffn(hp, op, w_o, ffn_norm[i], w_gu, w_d)
        hs = _oproj_ffn(hs, os_, w_o, ffn_norm[i], w_gu, w_d)

    return (hp.reshape(b, s, d), hs.reshape(n_seq, t_new, d),
            jnp.stack(outs["sb_kp"]), jnp.stack(outs["sb_vp"]),
            jnp.stack(outs["sb_ks"]), jnp.stack(outs["sb_vs"]),
            jnp.stack(outs["sw_kp"]), jnp.stack(outs["sw_vp"]),
            jnp.stack(outs["sw_ks"]), jnp.stack(outs["sw_vs"]))
```

```python
import functools

import jax
import jax.numpy as jnp
from jax import lax
from jax.experimental import pallas as pl
from jax.experimental.pallas import tpu as pltpu

F32 = jnp.float32
BF16 = jnp.bfloat16

HEAD_DIM = 64
N_HEADS = 16
N_KV = 4
GROUP = N_HEADS // N_KV
KV_WIDTH = N_KV * HEAD_DIM
Q_WIDTH = N_HEADS * HEAD_DIM
WINDOW = 128
ROT_DIM = HEAD_DIM // 4
ROPE_THETA = 500000.0
EPS = 1e-6
PAGE_SIZE = 128
SCALE = HEAD_DIM ** -0.5
LOG2E = 1.4426950408889634

LANES = 128
BLK = 128
SB_TQ = 512
SB_TK = 256
SW_QBLOCKS = 8
BIAS_TERMS = 3
ROW_TILE = 512
VMEM_LIMIT = 56 * 1024 * 1024
CHUNK_SLOTS = 3
PAGES_PER_CHUNK = 16
SEQS_PER_STEP = 8


def _params(sem):
    return pltpu.CompilerParams(dimension_semantics=sem, vmem_limit_bytes=VMEM_LIMIT)


def _const_spec(shape):
    nd = len(shape)
    return pl.BlockSpec(shape, lambda *_: (0,) * nd, pipeline_mode=pl.Buffered(1))


def _rms_norm_rows(x, gain):
    ms = jnp.mean(x * x, axis=-1, keepdims=True)
    return x * lax.rsqrt(ms + EPS) * gain


def _softplus(z):
    return jnp.maximum(z, 0.0) + jnp.log(1.0 + jnp.exp2(jnp.abs(z) * (-LOG2E)))


def _dot_nt(a, b):
    return lax.dot_general(a, b, (((1,), (1,)), ((), ())), preferred_element_type=F32)


def _dot(a, b):
    return jnp.dot(a, b, preferred_element_type=F32)


def _store_head_major(dst_ref, x):
    for h in range(N_KV):
        dst_ref[h] = x[:, h * HEAD_DIM:(h + 1) * HEAD_DIM].astype(BF16)


def _sb_qkv_kernel(x_ref, g_ref, w_ref, q_ref, k_ref, v_ref, khm_ref, vhm_ref):
    h = _rms_norm_rows(x_ref[...], g_ref[...]).astype(BF16)
    qkv = _dot(h, w_ref[...])
    q_ref[...] = (qkv[:, :Q_WIDTH] * SCALE).astype(BF16)
    k = qkv[:, Q_WIDTH:Q_WIDTH + KV_WIDTH]
    v = qkv[:, Q_WIDTH + KV_WIDTH:]
    k_ref[...] = k
    v_ref[...] = v
    lane = lax.broadcasted_iota(jnp.int32, (k.shape[0], HEAD_DIM), 1)
    ones_cols = jnp.where(lane < BIAS_TERMS, 1.0, 0.0).astype(BF16)
    for h in range(N_KV):
        khm_ref[h, :, :HEAD_DIM] = k[:, h * HEAD_DIM:(h + 1) * HEAD_DIM].astype(BF16)
        khm_ref[h, :, HEAD_DIM:] = ones_cols
    _store_head_major(vhm_ref, v)


def _split_bf16(x):
    hi = x.astype(BF16)
    lo = (x - hi.astype(F32)).astype(BF16)
    return hi, lo


def _sw_qkv_kernel(x_ref, g_ref, w_ref, gain_ref, psum_ref, pexp_ref, cos_ref, s1_ref, s2_ref,
                   q_ref, k_ref, v_ref, khm_ref, vhm_ref):
    h = _rms_norm_rows(x_ref[...], g_ref[...]).astype(BF16)
    qkv = _dot(h, w_ref[...])
    qk_width = Q_WIDTH + KV_WIDTH
    qk = qkv[:, :qk_width]
    v = qkv[:, qk_width:]
    sq_hi, sq_lo = _split_bf16(qk * qk)
    ss = _dot(sq_hi, psum_ref[...]) + _dot(sq_lo, psum_ref[...])
    inv_hi, inv_lo = _split_bf16(lax.rsqrt(ss * (1.0 / HEAD_DIM) + EPS))
    inv = _dot(inv_hi, pexp_ref[...]) + _dot(inv_lo, pexp_ref[...])
    qkn = qk * inv * gain_ref[...]
    cos, s1, s2 = cos_ref[...], s1_ref[...], s2_ref[...]
    half = ROT_DIM // 2
    for c in range(qk_width // LANES):
        blk = qkn[:, c * LANES:(c + 1) * LANES]
        rot = (blk * cos + pltpu.roll(blk, half, axis=1) * s1
               + pltpu.roll(blk, LANES - half, axis=1) * s2)
        if c < Q_WIDTH // LANES:
            q_ref[:, c * LANES:(c + 1) * LANES] = (rot * SCALE).astype(BF16)
        else:
            j = c - Q_WIDTH // LANES
            k_ref[:, j * LANES:(j + 1) * LANES] = rot
            khm_ref[2 * j] = rot[:, :HEAD_DIM].astype(BF16)
            khm_ref[2 * j + 1] = rot[:, HEAD_DIM:].astype(BF16)
    v_ref[...] = v
    ones_cols = jnp.ones((v.shape[0], HEAD_DIM), BF16)
    for h in range(N_KV):
        vhm_ref[h, :, :HEAD_DIM] = v[:, h * HEAD_DIM:(h + 1) * HEAD_DIM].astype(BF16)
        vhm_ref[h, :, HEAD_DIM:] = ones_cols


def _qkv_out(n_rows, tm, khm_width, vhm_width):
    shapes = (jax.ShapeDtypeStruct((n_rows, Q_WIDTH), BF16),
              jax.ShapeDtypeStruct((n_rows, KV_WIDTH), F32),
              jax.ShapeDtypeStruct((n_rows, KV_WIDTH), F32),
              jax.ShapeDtypeStruct((N_KV, n_rows, khm_width), BF16),
              jax.ShapeDtypeStruct((N_KV, n_rows, vhm_width), BF16))
    specs = (pl.BlockSpec((tm, Q_WIDTH), lambda i: (i, 0)),
             pl.BlockSpec((tm, KV_WIDTH), lambda i: (i, 0)),
             pl.BlockSpec((tm, KV_WIDTH), lambda i: (i, 0)),
             pl.BlockSpec((N_KV, tm, khm_width), lambda i: (0, i, 0)),
             pl.BlockSpec((N_KV, tm, vhm_width), lambda i: (0, i, 0)))
    return shapes, specs


def _row_tile(n_rows):
    return min(ROW_TILE, n_rows)


def _sb_qkv(x, gain, w):
    n_rows, d = x.shape
    tm = _row_tile(n_rows)
    shapes, specs = _qkv_out(n_rows, tm, 2 * HEAD_DIM, HEAD_DIM)
    return pl.pallas_call(
        _sb_qkv_kernel,
        grid=(n_rows // tm,),
        in_specs=[pl.BlockSpec((tm, d), lambda i: (i, 0)),
                  _const_spec((1, d)),
                  _const_spec(w.shape)],
        out_specs=specs, out_shape=shapes,
        compiler_params=_params(("arbitrary",)),
        name="sb_qkv",
    )(x, gain.reshape(1, d), w)


def _rope_tables(pos):
    half = ROT_DIM // 2
    inv_freq = ROPE_THETA ** (-jnp.arange(half, dtype=F32) * 2.0 / ROT_DIM)
    ang = pos.astype(F32)[:, None] * inv_freq[None, :]
    cos, sin = jnp.cos(ang), jnp.sin(ang)
    t = pos.shape[0]
    ones = jnp.ones((t, HEAD_DIM - ROT_DIM), F32)
    zeros_h = jnp.zeros((t, half), F32)
    zeros_r = jnp.zeros((t, HEAD_DIM - ROT_DIM), F32)
    c_head = jnp.concatenate([cos, cos, ones], axis=1)
    s1_head = jnp.concatenate([zeros_h, sin, zeros_r], axis=1)
    s2_head = jnp.concatenate([-sin, zeros_h, zeros_r], axis=1)
    two = lambda a: jnp.concatenate([a, a], axis=1)
    return two(c_head), two(s1_head), two(s2_head)


def _sw_qkv(x, gain, w, q_norm, k_norm, pos_tables, table_blocks):
    n_rows, d = x.shape
    tm = _row_tile(n_rows)
    shapes, specs = _qkv_out(n_rows, tm, HEAD_DIM, 2 * HEAD_DIM)
    qk_width = Q_WIDTH + KV_WIDTH
    head_of_lane = jnp.arange(qk_width) // HEAD_DIM
    psum = (head_of_lane[:, None] == jnp.arange(LANES)[None, :]).astype(BF16)
    pexp = psum.T
    gain_qk = jnp.concatenate([jnp.tile(q_norm, N_HEADS), jnp.tile(k_norm, N_KV)]).reshape(1, qk_width)
    tab_spec = pl.BlockSpec((tm, LANES), lambda i: (i % table_blocks, 0))
    return pl.pallas_call(
        _sw_qkv_kernel,
        grid=(n_rows // tm,),
        in_specs=[pl.BlockSpec((tm, d), lambda i: (i, 0)),
                  _const_spec((1, d)),
                  _const_spec(w.shape),
                  _const_spec((1, qk_width)),
                  _const_spec(psum.shape),
                  _const_spec(pexp.shape),
                  tab_spec, tab_spec, tab_spec],
        out_specs=specs, out_shape=shapes,
        compiler_params=_params(("arbitrary",)),
        name="sw_qkv",
    )(x, gain.reshape(1, d), w, gain_qk.astype(F32), psum, pexp, *pos_tables)


def _oproj_ffn_kernel(h_ref, o_ref, wo_ref, g_ref, wgu_ref, wd_ref, out_ref, *, d_ff, ff_chunk):
    h1 = h_ref[...] + _dot(o_ref[...], wo_ref[...])
    n = _rms_norm_rows(h1, g_ref[...]).astype(BF16)
    acc = h1
    for c in range(d_ff // ff_chunk):
        lo = c * ff_chunk
        gate = _dot(n, wgu_ref[:, lo:lo + ff_chunk])
        up = _dot(n, wgu_ref[:, d_ff + lo:d_ff + lo + ff_chunk])
        act = (gate * jax.nn.sigmoid(gate) * up).astype(BF16)
        acc = acc + _dot(act, wd_ref[lo:lo + ff_chunk, :])
    out_ref[...] = acc


def _ff_chunk(d_ff):
    for n_chunks in (2, 1):
        if d_ff % (n_chunks * LANES) == 0:
            return d_ff // n_chunks
    return d_ff


def _oproj_ffn(h, o, wo, gain, wgu, wd):
    n_rows, d = h.shape
    d_ff = wd.shape[0]
    tm = _row_tile(n_rows)
    kern = functools.partial(_oproj_ffn_kernel, d_ff=d_ff, ff_chunk=_ff_chunk(d_ff))
    return pl.pallas_call(
        kern,
        grid=(n_rows // tm,),
        in_specs=[pl.BlockSpec((tm, d), lambda i: (i, 0)),
                  pl.BlockSpec((tm, o.shape[1]), lambda i: (i, 0)),
                  _const_spec(wo.shape),
                  _const_spec((1, d)),
                  _const_spec(wgu.shape),
                  _const_spec(wd.shape)],
        out_specs=pl.BlockSpec((tm, d), lambda i: (i, 0)),
        out_shape=jax.ShapeDtypeStruct((n_rows, d), F32),
        compiler_params=_params(("arbitrary",)),
        name="oproj_ffn",
    )(h, o, wo, gain.reshape(1, d), wgu, wd)


def _sb_prompt_kernel(q_ref, k_ref, v_ref, bcol_ref, tri_ref, o_ref, qs_ref, acc_ref, r_ref,
                      z_buf, s_buf, a_buf):
    i = pl.program_id(2)
    tq = q_ref.shape[0]
    m = GROUP * tq
    q = q_ref[...]
    for g in range(GROUP):
        qs_ref[g * tq:(g + 1) * tq, :HEAD_DIM] = q[:, g * HEAD_DIM:(g + 1) * HEAD_DIM]
    qs_ref[:, HEAD_DIM:] = bcol_ref[...]
    tri = tri_ref[...]

    def visit(j, first, key_offset):
        start = pl.multiple_of(j * SB_TK, SB_TK)
        z = _dot_nt(qs_ref[...], k_ref[pl.ds(start, SB_TK), :])
        sp = _softplus(z)
        if key_offset is not None:
            t = lax.broadcasted_iota(jnp.int32, (m, SB_TK), 0) & (tq - 1)
            s = lax.broadcasted_iota(jnp.int32, (m, SB_TK), 1) + key_offset
            valid = s < t
            sp = jnp.where(valid, sp, 0.0)
        cum = _dot(sp.astype(BF16), tri)
        c = cum if first else cum + r_ref[...]
        a = jnp.exp(z - c)
        if key_offset is not None:
            a = jnp.where(valid, a, 0.0)
        av = _dot(a.astype(BF16), v_ref[pl.ds(start, SB_TK), :])
        if first:
            acc_ref[...] = av
            r_ref[...] = cum[:, 0:1]
        else:
            acc_ref[...] += av
            r_ref[...] += cum[:, 0:1]

    per_q = tq // SB_TK
    for kb in reversed(range(per_q)):
        visit(i * per_q + kb, kb == per_q - 1, kb * SB_TK)

    def key_rows(j):
        return pl.ds(pl.multiple_of(j * SB_TK, SB_TK), SB_TK)

    def stage_a(j, slot):
        z = _dot_nt(qs_ref[...], k_ref[key_rows(j), :])
        z_buf[slot] = z
        s_buf[slot] = _softplus(z).astype(BF16)

    def stage_b(slot):
        cum = _dot(s_buf[slot], tri)
        a_buf[slot] = jnp.exp(z_buf[slot] - (cum + r_ref[...])).astype(BF16)
        r_ref[...] += cum[:, 0:1]

    def stage_c(j, slot):
        acc_ref[...] += _dot(a_buf[slot], v_ref[key_rows(j), :])

    @pl.when(i > 0)
    def _():
        n_full = i * per_q
        blk = lambda tau: n_full - 1 - tau
        stage_a(blk(0), 0)
        stage_a(blk(1), 1)
        stage_b(0)

        def body(d, carry):
            tau = 2 + 2 * d
            stage_c(blk(tau - 2), 0)
            stage_b(1)
            stage_a(blk(tau), 0)
            stage_c(blk(tau - 1), 1)
            stage_b(0)
            stage_a(blk(tau + 1), 1)
            return carry

        lax.fori_loop(0, n_full // 2 - 1, body, 0)
        stage_c(blk(n_full - 2), 0)
        stage_b(1)
        stage_c(blk(n_full - 1), 1)

    acc = acc_ref[...]
    o_ref[...] = jnp.concatenate(
        [acc[g * tq:(g + 1) * tq, :] for g in range(GROUP)], axis=1).astype(BF16)


def _bias_columns(bias, rows_per_head):
    b = bias.astype(F32)
    hi = b.astype(BF16)
    mid = (b - hi.astype(F32)).astype(BF16)
    lo = (b - hi.astype(F32) - mid.astype(F32)).astype(BF16)
    cols = jnp.stack([hi, mid, lo], axis=-1)
    cols = jnp.pad(cols, ((0, 0), (0, HEAD_DIM - BIAS_TERMS)))
    cols = jnp.broadcast_to(cols.reshape(N_KV, GROUP, 1, HEAD_DIM), (N_KV, GROUP, rows_per_head, HEAD_DIM))
    return cols.reshape(N_KV, GROUP * rows_per_head, HEAD_DIM)


def _tri(n):
    j = jnp.arange(n)[:, None]
    s = jnp.arange(n)[None, :]
    return (j >= s).astype(BF16)


def _bias_rows(bias, rows_per_head):
    b = bias.astype(F32).reshape(N_KV, GROUP, 1, 1)
    return jnp.broadcast_to(b, (N_KV, GROUP, rows_per_head, BLK)).reshape(N_KV, GROUP * rows_per_head, BLK)


def _sb_prompt_attn(q, khm, vhm, bias):
    b, s, _ = q.shape
    gw = GROUP * HEAD_DIM
    tq = SB_TQ
    m = GROUP * tq
    return pl.pallas_call(
        _sb_prompt_kernel,
        grid=(b, N_KV, s // tq),
        in_specs=[pl.BlockSpec((None, tq, gw), lambda bi, h, i: (bi, i, h)),
                  pl.BlockSpec((None, None, s, 2 * HEAD_DIM), lambda bi, h, i: (h, bi, 0, 0)),
                  pl.BlockSpec((None, None, s, HEAD_DIM), lambda bi, h, i: (h, bi, 0, 0)),
                  pl.BlockSpec((None, m, HEAD_DIM), lambda bi, h, i: (h, 0, 0)),
                  _const_spec((SB_TK, SB_TK))],
        out_specs=pl.BlockSpec((None, tq, gw), lambda bi, h, i: (bi, i, h)),
        out_shape=jax.ShapeDtypeStruct((b, s, Q_WIDTH), BF16),
        scratch_shapes=[pltpu.VMEM((m, 2 * HEAD_DIM), BF16),
                        pltpu.VMEM((m, HEAD_DIM), F32),
                        pltpu.VMEM((m, 1), F32),
                        pltpu.VMEM((2, m, SB_TK), F32),
                        pltpu.VMEM((2, m, SB_TK), BF16),
                        pltpu.VMEM((2, m, SB_TK), BF16)],
        compiler_params=_params(("arbitrary", "arbitrary", "arbitrary")),
        name="sb_prompt_attn",
    )(q, khm, vhm, _bias_columns(bias, tq), _tri(SB_TK))


def _block_diag_q(q, n_seq, t):
    qh = q.reshape(n_seq, t, N_KV, GROUP, HEAD_DIM).transpose(0, 2, 3, 1, 4)
    qh = qh.reshape(n_seq, N_KV, GROUP * t, 1, HEAD_DIM)
    sel = (jnp.arange(N_KV)[:, None] == jnp.arange(N_KV)[None, :]).astype(q.dtype)
    out = qh * sel[None, :, None, :, None]
    return out.reshape(n_seq, N_HEADS * t, KV_WIDTH)


def _take_block_diag(acc, rows_per_kv):
    row_h = lax.broadcasted_iota(jnp.int32, (acc.shape[0], HEAD_DIM), 0) // rows_per_kv
    out = acc[:, :HEAD_DIM]
    for h in range(1, N_KV):
        out = jnp.where(row_h == h, acc[:, h * HEAD_DIM:(h + 1) * HEAD_DIM], out)
    return out


def _unstack_heads(o, n_seq, t):
    return o.reshape(n_seq, N_KV, GROUP, t, HEAD_DIM).transpose(0, 3, 1, 2, 4).reshape(n_seq * t, Q_WIDTH)


def _sb_sample_kernel(pt_ref, q_ref, bias_ref, knew_ref, vnew_ref, tri_ref, ck_ref, cv_ref, o_ref,
                      kbuf, vbuf, sem, *, n_seq, n_pages, pages_per_chunk, t_new):
    b = pl.program_id(0)
    n_chunks = n_pages // pages_per_chunk
    rows = q_ref.shape[0]

    total_chunks = n_seq * n_chunks
    lookahead = CHUNK_SLOTS - 1

    def page_copies(number, g):
        seq = lax.div(number, n_chunks)
        chunk = lax.rem(number, n_chunks)
        slot = lax.rem(number, CHUNK_SLOTS)
        page = pt_ref[seq * n_pages + (n_pages - 1 - (chunk * pages_per_chunk + g))]
        return (pltpu.make_async_copy(ck_ref.at[page], kbuf.at[slot, g], sem.at[0, slot]),
                pltpu.make_async_copy(cv_ref.at[page], vbuf.at[slot, g], sem.at[1, slot]))

    def start_chunk(number):
        for g in range(pages_per_chunk):
            k_copy, v_copy = page_copies(number, g)
            k_copy.start()
            v_copy.start(priority=1)

    def wait_chunk(number):
        for g in range(pages_per_chunk):
            for cp in page_copies(number, g):
                cp.wait()

    @pl.when(b == 0)
    def _():
        for number in range(min(lookahead, total_chunks)):
            start_chunk(jnp.int32(number))

    q = q_ref[...]
    bias = bias_ref[...]
    tri = tri_ref[...]

    def visit(kt_tiles, vt_tiles, r, valid):
        zs = [_dot(q, kt.astype(BF16)) + bias for kt in kt_tiles]
        sps = [_softplus(z) for z in zs]
        if valid is not None:
            sps = [jnp.where(valid, sp, 0.0) for sp in sps]
        cums = [_dot(sp.astype(BF16), tri) for sp in sps]
        av = None
        for z, cum, vt in zip(zs, cums, vt_tiles):
            a = jnp.exp(z - (cum + r))
            if valid is not None:
                a = jnp.where(valid, a, 0.0)
            term = _dot_nt(a.astype(BF16), vt.astype(BF16))
            av = term if av is None else av + term
            r = r + cum[:, 0:1]
        return av, r

    t_of_row = lax.broadcasted_iota(jnp.int32, (rows, BLK), 0) % t_new
    col = lax.broadcasted_iota(jnp.int32, (rows, BLK), 1)
    acc0, r0 = visit([knew_ref[...]], [vnew_ref[...]], jnp.zeros((rows, 1), F32), col < t_of_row)

    def chunk_body(chunk, carry):
        acc, r = carry
        number = b * n_chunks + chunk
        slot = lax.rem(number, CHUNK_SLOTS)

        @pl.when(number + lookahead < total_chunks)
        def _():
            start_chunk(number + lookahead)

        wait_chunk(number)
        av, r = visit([kbuf[slot, g] for g in range(pages_per_chunk)],
                      [vbuf[slot, g] for g in range(pages_per_chunk)], r, None)
        return acc + av, r

    acc, _ = lax.fori_loop(0, n_chunks, chunk_body, (acc0, r0))
    o_ref[...] = _take_block_diag(acc, rows // N_KV).astype(BF16)


def _pad_rows(x, rows):
    return jnp.pad(x, ((0, 0), (0, rows - x.shape[1]), (0, 0)))


def _transposed_page(x, n_seq, t):
    xt = x.reshape(n_seq, t, KV_WIDTH).transpose(0, 2, 1)
    return jnp.pad(xt, ((0, 0), (0, 0), (0, BLK - t)))


def _sb_sample_attn(q, k_new, v_new, bias, cache_kt, cache_vt, page_table):
    n_seq, n_pages = page_table.shape
    t = q.shape[0] // n_seq
    rows = N_HEADS * t
    qbd = _block_diag_q(q, n_seq, t)
    bias_rows = _bias_rows(bias, t).reshape(rows, BLK)
    ppc = PAGES_PER_CHUNK if n_pages % PAGES_PER_CHUNK == 0 else 1
    kern = functools.partial(_sb_sample_kernel, n_seq=n_seq, n_pages=n_pages, pages_per_chunk=ppc, t_new=t)
    grid_spec = pltpu.PrefetchScalarGridSpec(
        num_scalar_prefetch=1,
        grid=(n_seq,),
        in_specs=[pl.BlockSpec((None, rows, KV_WIDTH), lambda s, pt: (s, 0, 0)),
                  pl.BlockSpec((rows, BLK), lambda s, pt: (0, 0)),
                  pl.BlockSpec((None, KV_WIDTH, BLK), lambda s, pt: (s, 0, 0)),
                  pl.BlockSpec((None, KV_WIDTH, BLK), lambda s, pt: (s, 0, 0)),
                  pl.BlockSpec((BLK, BLK), lambda s, pt: (0, 0)),
                  pl.BlockSpec(memory_space=pl.ANY),
                  pl.BlockSpec(memory_space=pl.ANY)],
        out_specs=pl.BlockSpec((None, rows, HEAD_DIM), lambda s, pt: (s, 0, 0)),
        scratch_shapes=[pltpu.VMEM((CHUNK_SLOTS, ppc, KV_WIDTH, PAGE_SIZE), F32),
                        pltpu.VMEM((CHUNK_SLOTS, ppc, KV_WIDTH, PAGE_SIZE), F32),
                        pltpu.SemaphoreType.DMA((2, CHUNK_SLOTS))])
    o = pl.pallas_call(
        kern,
        grid_spec=grid_spec,
        out_shape=jax.ShapeDtypeStruct((n_seq, rows, HEAD_DIM), BF16),
        compiler_params=_params(("arbitrary",)),
        name="sb_sample_attn",
    )(page_table.reshape(-1), qbd, bias_rows, _transposed_page(k_new, n_seq, t),
      _transposed_page(v_new, n_seq, t), _tri(BLK), cache_kt, cache_vt)
    return _unstack_heads(o, n_seq, t)


def _sink_softmax_av(scores, values, sink):
    mx = sink
    for s in scores:
        mx = jnp.maximum(mx, jnp.max(s, axis=-1, keepdims=True))
    ps = [jnp.exp(s - mx) for s in scores]
    den = jnp.exp(sink - mx)
    for p in ps:
        den = den + jnp.sum(p, axis=-1, keepdims=True)
    inv = 1.0 / den
    out = None
    for p, v in zip(ps, values):
        term = _dot((p * inv).astype(BF16), v)
        out = term if out is None else out + term
    return out


def _sw_prompt_kernel(q_ref, kc_ref, kp_ref, vc_ref, vp_ref, sink_ref, sinkb_ref, o_ref, qs_ref):
    i = pl.program_id(2)
    m = GROUP * BLK
    q = q_ref[...]
    for sub in range(SW_QBLOCKS):
        for g in range(GROUP):
            r0 = (sub * GROUP + g) * BLK
            qs_ref[r0:r0 + BLK, :] = q[sub * BLK:(sub + 1) * BLK, g * HEAD_DIM:(g + 1) * HEAD_DIM]
    t = lax.broadcasted_iota(jnp.int32, (m, BLK), 0) % BLK
    s = lax.broadcasted_iota(jnp.int32, (m, BLK), 1)
    sink = sink_ref[...]
    sinkb = sinkb_ref[...]
    neg = -jnp.inf

    def own(ref, sub):
        return ref[sub * BLK:(sub + 1) * BLK, :]

    def before(ref, prev_ref, sub):
        return prev_ref[...] if sub == 0 else own(ref, sub - 1)

    def scores(sub):
        qs = qs_ref[sub * m:(sub + 1) * m, :]
        s_cur = jnp.where(s <= t, _dot_nt(qs, own(kc_ref, sub)), neg)
        first_key = t + jnp.where(i > 0, 0, BLK) if sub == 0 else t
        s_prev = jnp.where(s >= first_key, _dot_nt(qs, before(kc_ref, kp_ref, sub)), neg)
        row_max = jnp.max(jnp.maximum(s_prev, s_cur), axis=-1, keepdims=True)
        return s_prev, s_cur, jnp.broadcast_to(jnp.maximum(sink, row_max), (m, BLK))

    def finish(sub, s_prev, s_cur, mx):
        av = (_dot(jnp.exp(s_prev - mx).astype(BF16), before(vc_ref, vp_ref, sub))
              + _dot(jnp.exp(s_cur - mx).astype(BF16), own(vc_ref, sub)))
        den = pltpu.roll(av, HEAD_DIM, axis=1) + jnp.exp(sinkb - mx)
        out = (av * (1.0 / den))[:, :HEAD_DIM]
        o_ref[sub * BLK:(sub + 1) * BLK, :] = jnp.concatenate(
            [out[g * BLK:(g + 1) * BLK, :] for g in range(GROUP)], axis=1).astype(BF16)

    pending = None
    for sub in range(SW_QBLOCKS):
        scored = scores(sub)
        if pending is not None:
            finish(*pending)
        pending = (sub,) + scored
    finish(*pending)


def _sink_rows(sinks, rows_per_head):
    sk = sinks.astype(F32).reshape(N_KV, GROUP, 1, 1)
    return jnp.broadcast_to(sk, (N_KV, GROUP, rows_per_head, 1)).reshape(N_KV, GROUP * rows_per_head, 1)


def _sw_prompt_attn(q, khm, vhm, sinks):
    b, s, _ = q.shape
    gw = GROUP * HEAD_DIM
    m = GROUP * BLK
    tq = SW_QBLOCKS * BLK
    own = lambda width: pl.BlockSpec((None, None, tq, width), lambda bi, h, i: (h, bi, i, 0))
    before = lambda width: pl.BlockSpec((None, None, BLK, width),
                                        lambda bi, h, i: (h, bi, jnp.maximum(i * SW_QBLOCKS - 1, 0), 0))
    sink_rows = _sink_rows(sinks, BLK)
    return pl.pallas_call(
        _sw_prompt_kernel,
        grid=(b, N_KV, s // tq),
        in_specs=[pl.BlockSpec((None, tq, gw), lambda bi, h, i: (bi, i, h)),
                  own(HEAD_DIM), before(HEAD_DIM), own(2 * HEAD_DIM), before(2 * HEAD_DIM),
                  pl.BlockSpec((None, m, 1), lambda bi, h, i: (h, 0, 0)),
                  pl.BlockSpec((None, m, BLK), lambda bi, h, i: (h, 0, 0))],
        out_specs=pl.BlockSpec((None, tq, gw), lambda bi, h, i: (bi, i, h)),
        out_shape=jax.ShapeDtypeStruct((b, s, Q_WIDTH), BF16),
        scratch_shapes=[pltpu.VMEM((SW_QBLOCKS * m, HEAD_DIM), BF16)],
        compiler_params=_params(("arbitrary", "arbitrary", "arbitrary")),
        name="sw_prompt_attn",
    )(q, khm, khm, vhm, vhm, sink_rows, jnp.broadcast_to(sink_rows, (N_KV, m, BLK)))


def _sw_sample_kernel(q_ref, kk_ref, vv_ref, sink_ref, o_ref, *, t_new, n_buf):
    rows = q_ref.shape[1]
    n_keys = kk_ref.shape[1]
    t = lax.broadcasted_iota(jnp.int32, (rows, n_keys), 0) % t_new
    j = lax.broadcasted_iota(jnp.int32, (rows, n_keys), 1)
    valid = jnp.logical_and(j >= n_buf + t - WINDOW, j <= n_buf + t)
    sink = sink_ref[...]
    for i in range(q_ref.shape[0]):
        sc = jnp.where(valid, _dot_nt(q_ref[i], kk_ref[i].astype(BF16)), -jnp.inf)
        out = _sink_softmax_av([sc], [vv_ref[i].astype(BF16)], sink)
        o_ref[i] = _take_block_diag(out, rows // N_KV).astype(BF16)


def _sw_sample_attn(q, kk, vv, sinks, n_buf):
    n_seq = kk.shape[0]
    t = q.shape[0] // n_seq
    rows = N_HEADS * t
    n_keys = -(-kk.shape[1] // LANES) * LANES
    kk = _pad_rows(kk, n_keys)
    vv = _pad_rows(vv, n_keys)
    qbd = _block_diag_q(q, n_seq, t)
    sink_rows = _sink_rows(sinks, t).reshape(rows, 1)
    ns = SEQS_PER_STEP if n_seq % SEQS_PER_STEP == 0 else 1
    kern = functools.partial(_sw_sample_kernel, t_new=t, n_buf=n_buf)
    o = pl.pallas_call(
        kern,
        grid=(n_seq // ns,),
        in_specs=[pl.BlockSpec((ns, rows, KV_WIDTH), lambda s: (s, 0, 0)),
                  pl.BlockSpec((ns, n_keys, KV_WIDTH), lambda s: (s, 0, 0)),
                  pl.BlockSpec((ns, n_keys, KV_WIDTH), lambda s: (s, 0, 0)),
                  pl.BlockSpec((rows, 1), lambda s: (0, 0))],
        out_specs=pl.BlockSpec((ns, rows, HEAD_DIM), lambda s: (s, 0, 0)),
        out_shape=jax.ShapeDtypeStruct((n_seq, rows, HEAD_DIM), BF16),
        compiler_params=_params(("arbitrary",)),
        name="sw_sample_attn",
    )(qbd, kk, vv, sink_rows)
    return _unstack_heads(o, n_seq, t)


def kernel(x_prompt, x_sample, cache_k_sb, cache_v_sb, page_table, state_k_win, state_v_win,
           sb_norm, sb_w_qkv, sb_bias, sb_w_o, sw_norm, sw_w_qkv, sw_q_norm, sw_k_norm, sw_sinks, sw_w_o,
           ffn_norm, ffn_w_gate_up, ffn_w_down):
    b, s, d = x_prompt.shape
    n_seq, t_new, _ = x_sample.shape
    depth = ffn_norm.shape[0]
    past_len = page_table.shape[1] * PAGE_SIZE
    n_buf = state_k_win.shape[2]
    assert s % ROW_TILE == 0 and s % SB_TQ == 0 and SB_TQ == 2 * SB_TK

    hp = x_prompt.reshape(b * s, d)
    hs = x_sample.reshape(n_seq * t_new, d)
    bf = lambda w: w.astype(BF16)

    prompt_tables = _rope_tables(jnp.arange(s))
    sample_tables = _rope_tables(past_len + jnp.arange(t_new))
    sample_tables = tuple(jnp.tile(tab, (n_seq, 1)) for tab in sample_tables)

    outs = {name: [] for name in ("sb_kp", "sb_vp", "sb_ks", "sb_vs", "sw_kp", "sw_vp", "sw_ks", "sw_vs")}
    for i in range(depth):
        j = i // 2
        if i % 2 == 0:
            w_qkv = bf(sb_w_qkv[j])
            q, k, v, khm, vhm = _sb_qkv(hp, sb_norm[j], w_qkv)
            op = _sb_prompt_attn(q.reshape(b, s, Q_WIDTH), khm.reshape(N_KV, b, s, 2 * HEAD_DIM),
                                 vhm.reshape(N_KV, b, s, HEAD_DIM), sb_bias[j]).reshape(b * s, Q_WIDTH)
            qs, ks, vs, _, _ = _sb_qkv(hs, sb_norm[j], w_qkv)
            n_phys = cache_k_sb.shape[1]
            to_pages = lambda c: c.transpose(0, 2, 3, 1).reshape(n_phys, KV_WIDTH, PAGE_SIZE)
            os_ = _sb_sample_attn(qs, ks, vs, sb_bias[j], to_pages(cache_k_sb[j]), to_pages(cache_v_sb[j]),
                                  page_table)
            w_o = bf(sb_w_o[j])
            outs["sb_kp"].append(k.reshape(b, s, N_KV, HEAD_DIM))
            outs["sb_vp"].append(v.reshape(b, s, N_KV, HEAD_DIM))
            outs["sb_ks"].append(ks.reshape(n_seq, t_new, N_KV, HEAD_DIM))
            outs["sb_vs"].append(vs.reshape(n_seq, t_new, N_KV, HEAD_DIM))
        else:
            w_qkv = bf(sw_w_qkv[j])
            q, k, v, khm, vhm = _sw_qkv(hp, sw_norm[j], w_qkv, sw_q_norm[j], sw_k_norm[j],
                                        prompt_tables, s // _row_tile(b * s))
            op = _sw_prompt_attn(q.reshape(b, s, Q_WIDTH), khm.reshape(N_KV, b, s, HEAD_DIM),
                                 vhm.reshape(N_KV, b, s, 2 * HEAD_DIM), sw_sinks[j]).reshape(b * s, Q_WIDTH)
            qs, ks, vs, _, _ = _sw_qkv(hs, sw_norm[j], w_qkv, sw_q_norm[j], sw_k_norm[j],
                                       sample_tables, 1)
            kk = jnp.concatenate([state_k_win[j].reshape(n_seq, n_buf, KV_WIDTH),
                                  ks.reshape(n_seq, t_new, KV_WIDTH)], axis=1)
            vv = jnp.concatenate([state_v_win[j].reshape(n_seq, n_buf, KV_WIDTH),
                                  vs.reshape(n_seq, t_new, KV_WIDTH)], axis=1)
            os_ = _sw_sample_attn(qs, kk, vv, sw_sinks[j], n_buf)
            w_o = bf(sw_w_o[j])
            keep = min(WINDOW, s)
            last = lambda x: x.reshape(b, s, KV_WIDTH)[:, s - keep:, :].reshape(b, keep, N_KV, HEAD_DIM)
            outs["sw_kp"].append(last(k))
            outs["sw_vp"].append(last(v))
            outs["sw_ks"].append(kk[:, -n_buf:].reshape(n_seq, n_buf, N_KV, HEAD_DIM))
            outs["sw_vs"].append(vv[:, -n_buf:].reshape(n_seq, n_buf, N_KV, HEAD_DIM))
        w_gu, w_d = bf(ffn_w_gate_up[i]), bf(ffn_w_down[i])
        hp = _oproj_ffn(hp, op, w_o, ffn_norm[i], w_gu, w_d)
        hs = _oproj_ffn(hs, os_, w_o, ffn_norm[i], w_gu, w_d)

    return (hp.reshape(b, s, d), hs.reshape(n_seq, t_new, d),
            jnp.stack(outs["sb_kp"]), jnp.stack(outs["sb_vp"]),
            jnp.stack(outs["sb_ks"]), jnp.stack(outs["sb_vs"]),
            jnp.stack(outs["sw_kp"]), jnp.stack(outs["sw_vp"]),
            jnp.stack(outs["sw_ks"]), jnp.stack(outs["sw_vs"]))
```
